```python
import jax, jax.numpy as jnp
from jax import lax
import numpy as np

D_MODEL = 1024
BATCH = 8
SEQ = 4096
DEPTH = 2

GRID_W = 64
CTX_LEN = 256
HG_DIM = 128
HG_WIDTH = D_MODEL // 2
HG_HEADS = HG_WIDTH // HG_DIM
MLP_WIDTH = D_MODEL - HG_WIDTH
MLP_HEADS = 4
MLP_DIM = MLP_WIDTH // MLP_HEADS
MLP_CHUNK = 128
SCAN_CHUNK = 32
MIX_WIDTH = HG_WIDTH + MLP_WIDTH
PROJ_WIDTH = 5 * HG_WIDTH + 2 * MLP_WIDTH
D_FF = 2816
CONV_W = 3
EPS = 1e-6

kernel_name = "hybrid_hgrn2_gmlp_dit_prefix"


def rmsnorm(x, gain):
    xf = x.astype(jnp.float32)
    y = xf * lax.rsqrt(jnp.mean(xf * xf, axis=-1, keepdims=True) + EPS)
    return (y * gain.astype(jnp.float32)).astype(x.dtype)


def layernorm(x, gain, bias):
    xf = x.astype(jnp.float32)
    mu = jnp.mean(xf, axis=-1, keepdims=True)
    var = jnp.mean(jnp.square(xf - mu), axis=-1, keepdims=True)
    y = (xf - mu) * lax.rsqrt(var + EPS) * gain.astype(jnp.float32) + bias.astype(jnp.float32)
    return y.astype(x.dtype)


def modulate(h, shift, scale):
    return h * (1.0 + scale) + shift


def split_heads(a, n_heads):
    b, t, w = a.shape
    return a.reshape(b, t, n_heads, w // n_heads).transpose(0, 2, 1, 3)


def merge_heads(a):
    b, h, t, d = a.shape
    return a.transpose(0, 2, 1, 3).reshape(b, t, h * d)


def flip_t(a):
    return jnp.flip(a, axis=2)


def lower_bounds(logits):
    cum = jnp.cumsum(jax.nn.softmax(logits.astype(jnp.float32), axis=0), axis=0)
    return cum - cum[0:1]


def forget_gate(f_logit, lb, first):
    f = f_logit.astype(jnp.float32)
    if first:
        return jax.nn.log_sigmoid(f), jax.nn.sigmoid(-f)
    gate = lb + (1.0 - lb) * jax.nn.sigmoid(f)
    return jnp.log(gate), 1.0 - gate


def hgrn2_gates(p, lb_f, lb_b, first):
    w = HG_WIDTH
    logf_f, k_f = forget_gate(p[..., :w], lb_f, first)
    logf_b, k_b = forget_gate(p[..., w:2 * w], lb_b, first)
    v = p[..., 2 * w:3 * w].astype(jnp.float32)
    return tuple(split_heads(a, HG_HEADS) for a in (logf_f, k_f, logf_b, k_b, v))


def advance_state(h, k, v, b):
    b_last = b[:, :, -1:, :]
    return (jnp.exp(b_last)[:, :, 0, :, None] * h
            + jnp.einsum('bhsk,bhsv->bhkv', k * jnp.exp(b_last - b), v))


def chunk_gla(q, k, v, log_f, h0):
    b_, h_, t, _ = q.shape
    n = t // SCAN_CHUNK

    def to_chunks(a):
        return jnp.moveaxis(a.reshape(b_, h_, n, SCAN_CHUNK, a.shape[-1]), 2, 0)

    mask = jnp.tril(jnp.ones((SCAN_CHUNK, SCAN_CHUNK), dtype=bool))[:, :, None]

    def step(h, blk):
        qc, kc, vc, gc = blk
        b = jnp.cumsum(gc, axis=2)
        o_inter = jnp.einsum('bhtk,bhkv->bhtv', qc * jnp.exp(b), h)
        diff = b[:, :, :, None, :] - b[:, :, None, :, :]
        decay = jnp.where(mask, jnp.exp(jnp.where(mask, diff, 0.0)), 0.0)
        scores = jnp.einsum('bhtk,bhsk,bhtsk->bhts', qc, kc, decay)
        o = o_inter + jnp.einsum('bhts,bhsv->bhtv', scores, vc)
        return advance_state(h, kc, vc, b), o

    h_end, o = lax.scan(step, h0, (to_chunks(q), to_chunks(k), to_chunks(v), to_chunks(log_f)))
    o = jnp.moveaxis(o, 0, 2).reshape(b_, h_, t, v.shape[-1])
    return o, h_end


def chunk_sgu(u, v, ln_g, ln_b, w_s, b_s):
    bsz, t, _ = u.shape
    n = t // MLP_CHUNK
    u = jax.nn.gelu(u, approximate=False).reshape(bsz, n, MLP_CHUNK, MLP_HEADS, MLP_DIM)
    v = jax.nn.gelu(v, approximate=False).reshape(bsz, n, MLP_CHUNK, MLP_HEADS, MLP_DIM)
    v = layernorm(v, ln_g.reshape(MLP_HEADS, MLP_DIM), ln_b.reshape(MLP_HEADS, MLP_DIM))
    z = jnp.einsum('hpq,bnqhc->bnphc', w_s.astype(v.dtype), v) + jnp.transpose(b_s)[None, None, :, :, None]
    return (u * z).reshape(bsz, t, MLP_WIDTH)


def token_mixers(p, lb_f, lb_b, first, h0_f, h0_b, hg_gain, sgu_g, sgu_b, w_s, b_s):
    w = HG_WIDTH
    logf_f, k_f, logf_b, k_b, v = hgrn2_gates(p, lb_f, lb_b, first)
    q = split_heads(jax.nn.silu(p[..., 3 * w:4 * w].astype(jnp.float32)), HG_HEADS) * (HG_DIM ** -0.5)
    o_f, h_f = chunk_gla(q, k_f, v, logf_f, h0_f)
    o_b, h_b = chunk_gla(flip_t(q), flip_t(k_b), flip_t(v), flip_t(logf_b), h0_b)
    o = rmsnorm(o_f + flip_t(o_b), hg_gain.reshape(HG_HEADS, 1, HG_DIM))
    o_hg = merge_heads(o).astype(p.dtype) * jax.nn.silu(p[..., 4 * w:5 * w])
    o_mlp = chunk_sgu(p[..., 5 * w:5 * w + MLP_WIDTH], p[..., 5 * w + MLP_WIDTH:], sgu_g, sgu_b, w_s, b_s)
    return jnp.concatenate([o_hg, o_mlp], axis=-1), h_f, h_b


def conv_ffn(h, w_up, taps, conv_b, w_down, rows):
    a, g = jnp.split(h @ w_up, 2, axis=-1)
    bsz, t, f = g.shape
    if rows is None:
        g2, tp = g.reshape(bsz, 1, t, f), taps[1:2]
    else:
        g2, tp = g.reshape(bsz, rows, GRID_W, f), taps
    gc = lax.conv_general_dilated(g2, tp[:, :, None, :].astype(g.dtype), (1, 1), 'SAME',
                                  dimension_numbers=('NHWC', 'HWIO', 'NHWC'), feature_group_count=f)
    gc = gc.reshape(bsz, t, f) + conv_b
    return (a * jax.nn.gelu(gc, approximate=False)) @ w_down


def setup_inputs(seed: int = 0) -> dict:
    key = jax.random.key(seed)
    ks = jax.random.split(key, 22)
    nrm = jax.random.normal
    L, D, F = DEPTH, D_MODEL, D_FF
    return {
        "x": nrm(ks[0], (BATCH, SEQ, D), jnp.float32),
        "c": nrm(ks[1], (BATCH, D), jnp.float32),
        "ctx": nrm(ks[2], (BATCH, CTX_LEN, D), jnp.float32),
        "c_ctx": nrm(ks[3], (D,), jnp.float32),
        "w_ada": nrm(ks[4], (L, D, 6 * D), jnp.float32) * (0.5 * D ** -0.5),
        "b_ada": nrm(ks[5], (L, 6 * D), jnp.float32) * 0.02,
        "norm_mix": 1.0 + 0.1 * nrm(ks[6], (L, D), jnp.float32),
        "norm_ffn": 1.0 + 0.1 * nrm(ks[7], (L, D), jnp.float32),
        "w_in": nrm(ks[8], (L, D, PROJ_WIDTH), jnp.float32) * D ** -0.5,
        "lb_logits_fwd": nrm(ks[9], (L, HG_WIDTH), jnp.float32),
        "lb_logits_bwd": nrm(ks[10], (L, HG_WIDTH), jnp.float32),
        "hg_norm": 1.0 + 0.1 * nrm(ks[11], (L, HG_WIDTH), jnp.float32),
        "sgu_norm_g": 1.0 + 0.1 * nrm(ks[12], (L, MLP_WIDTH), jnp.float32),
        "sgu_norm_b": 0.02 * nrm(ks[13], (L, MLP_WIDTH), jnp.float32),
        "w_spatial": nrm(ks[14], (L, MLP_HEADS, MLP_CHUNK, MLP_CHUNK), jnp.float32) * MLP_CHUNK ** -0.5,
        "b_spatial": 1.0 + 0.1 * nrm(ks[15], (L, MLP_HEADS, MLP_CHUNK), jnp.float32),
        "w_out": nrm(ks[16], (L, MIX_WIDTH, D), jnp.float32) * MIX_WIDTH ** -0.5,
        "w_up": nrm(ks[17], (L, D, 2 * F), jnp.float32) * D ** -0.5,
        "conv_w": nrm(ks[18], (L, CONV_W, CONV_W, F), jnp.float32) / CONV_W,
        "conv_b": 0.02 * nrm(ks[19], (L, F), jnp.float32),
        "w_down": nrm(ks[20], (L, F, D), jnp.float32) * F ** -0.5,
        "norm_final": 1.0 + 0.1 * nrm(ks[21], (D,), jnp.float32),
    }


def reference(x, c, ctx, c_ctx, w_ada, b_ada, norm_mix, norm_ffn, w_in, lb_logits_fwd, lb_logits_bwd,
              hg_norm, sgu_norm_g, sgu_norm_b, w_spatial, b_spatial, w_out, w_up, conv_w, conv_b, w_down,
              norm_final):
    w = HG_WIDTH
    rows = x.shape[1] // GRID_W
    bsz = x.shape[0]
    lb_f_all = lower_bounds(lb_logits_fwd)
    lb_b_all = lower_bounds(lb_logits_bwd)
    ada = jnp.einsum('bd,lde->lbe', jax.nn.silu(c), w_ada) + b_ada[:, None, :]
    ada_c = jnp.einsum('d,lde->le', jax.nn.silu(c_ctx), w_ada) + b_ada
    h0 = jnp.zeros((bsz, HG_HEADS, HG_DIM, HG_DIM), jnp.float32)
    xc = ctx
    for l in range(DEPTH):
        first, last = l == 0, l == DEPTH - 1
        sh1, sc1, g1, sh2, sc2, g2 = (m[:, None, :] for m in jnp.split(ada[l], 6, axis=-1))
        csh1, csc1, cg1, csh2, csc2, cg2 = jnp.split(ada_c[l], 6, axis=-1)
        lb_f, lb_b = lb_f_all[l], lb_b_all[l]
        mix_args = (hg_norm[l], sgu_norm_g[l], sgu_norm_b[l], w_spatial[l], b_spatial[l])

        hc = modulate(rmsnorm(xc, norm_mix[l]), csh1, csc1)
        if last:
            logf_f, k_f, logf_b, k_b, vc = hgrn2_gates(hc @ w_in[l][:, :3 * w], lb_f, lb_b, first)
            hf_c = advance_state(h0, k_f, vc, jnp.cumsum(logf_f, axis=2))
            hb_c = advance_state(h0, flip_t(k_b), flip_t(vc), jnp.cumsum(flip_t(logf_b), axis=2))
        else:
            oc, hf_c, hb_c = token_mixers(hc @ w_in[l], lb_f, lb_b, first, h0, h0, *mix_args)

        h = modulate(rmsnorm(x, norm_mix[l]), sh1, sc1)
        o, _, _ = token_mixers(h @ w_in[l], lb_f, lb_b, first, hf_c, hb_c, *mix_args)
        x = x + g1 * (o @ w_out[l])
        h2 = modulate(rmsnorm(x, norm_ffn[l]), sh2, sc2)
        x = x + g2 * conv_ffn(h2, w_up[l], conv_w[l], conv_b[l], w_down[l], rows)

        if not last:
            xc = xc + cg1 * (oc @ w_out[l])
            hc2 = modulate(rmsnorm(xc, norm_ffn[l]), csh2, csc2)
            xc = xc + cg2 * conv_ffn(hc2, w_up[l], conv_w[l], conv_b[l], w_down[l], None)
    return rmsnorm(x, norm_final)
```

```python
import functools

import jax
import jax.numpy as jnp
from jax import lax
from jax.experimental import pallas as pl
from jax.experimental.pallas import tpu as pltpu

F32 = jnp.float32
BF16 = jnp.bfloat16

HG_DIM = 128
HG_HEADS = 4
HG_WIDTH = HG_DIM * HG_HEADS
MLP_HEADS = 4
MLP_DIM = 128
MLP_WIDTH = MLP_HEADS * MLP_DIM
GRID_W = 64
CHUNK = 128
EPS = 1e-6
Q_SCALE = HG_DIM ** -0.5
SQRT_HALF = 0.7071067811865476

FFN_CHUNK = 256
HALO_PAD = 8
VMEM_LIMIT = 56 * 1024 * 1024


def _const_spec(shape):
    zeros = (0,) * len(shape)
    return pl.BlockSpec(shape, lambda *_: zeros, pipeline_mode=pl.Buffered(1))


def _params(*semantics):
    return pltpu.CompilerParams(dimension_semantics=semantics, vmem_limit_bytes=VMEM_LIMIT)


def _gelu(x):
    return 0.5 * x * (1.0 + lax.erf(x * SQRT_HALF))


def _silu(x):
    return x * jax.nn.sigmoid(x)


def _dot(a, b):
    return jnp.dot(a, b, preferred_element_type=F32)


def _dot_nt(a, b):
    return lax.dot_general(a, b, (((1,), (1,)), ((), ())), preferred_element_type=F32)


def _split3(a):
    hi = a.astype(BF16)
    r = a - hi.astype(F32)
    mid = r.astype(BF16)
    lo = (r - mid.astype(F32)).astype(BF16)
    return hi, mid, lo


def _ada_kernel(c_ref, w_ref, b_ref, o_ref):
    a = _silu(c_ref[...])
    a_hi = a.astype(BF16)
    a_lo = (a - a_hi.astype(F32)).astype(BF16)
    w = w_ref[0]
    w_hi = w.astype(BF16)
    w_lo = (w - w_hi.astype(F32)).astype(BF16)
    o_ref[0] = _dot(a_hi, w_hi) + _dot(a_hi, w_lo) + _dot(a_lo, w_hi) + b_ref[0]


def _ada_call(cc, w_ada, b_ada):
    depth, d, e = w_ada.shape
    rows = cc.shape[0]
    tn = 1536
    return pl.pallas_call(
        _ada_kernel,
        grid=(depth, e // tn),
        in_specs=[
            pl.BlockSpec((rows, d), lambda l, j: (0, 0)),
            pl.BlockSpec((1, d, tn), lambda l, j: (l, 0, j)),
            pl.BlockSpec((1, 1, tn), lambda l, j: (l, 0, j)),
        ],
        out_specs=pl.BlockSpec((1, rows, tn), lambda l, j: (l, 0, j)),
        out_shape=jax.ShapeDtypeStruct((depth, rows, e), F32),
        compiler_params=_params("parallel", "parallel"),
        name="ada",
    )(cc, w_ada, b_ada.reshape(depth, 1, e))


def _in_proj_kernel(x_ref, sh_ref, sc_ref, gain_ref, w_ref, o_ref, *, col_tile):
    x = x_ref[0]
    ms = jnp.mean(x * x, axis=-1, keepdims=True)
    h = x * lax.rsqrt(ms + EPS) * gain_ref[...]
    h = (h * (1.0 + sc_ref[0]) + sh_ref[0]).astype(BF16)
    for j in range(w_ref.shape[1] // col_tile):
        cols = slice(j * col_tile, (j + 1) * col_tile)
        o_ref[0, :, cols] = _dot(h, w_ref[:, cols])


def _in_proj_call(x, shift, scale, gain, w, ncols):
    b, t, d = x.shape
    tm = min(256, t)
    per_batch = shift.shape[0] == b and b > 1
    mod_map = (lambda i, j: (i, 0, 0)) if per_batch else (lambda i, j: (0, 0, 0))
    return pl.pallas_call(
        functools.partial(_in_proj_kernel, col_tile=512),
        grid=(b, t // tm),
        in_specs=[
            pl.BlockSpec((1, tm, d), lambda i, j: (i, j, 0)),
            pl.BlockSpec((1, 1, d), mod_map),
            pl.BlockSpec((1, 1, d), mod_map),
            _const_spec((1, d)),
            pl.BlockSpec((d, ncols), lambda i, j: (0, 0), pipeline_mode=pl.Buffered(1)),
        ],
        out_specs=pl.BlockSpec((1, tm, ncols), lambda i, j: (i, j, 0)),
        out_shape=jax.ShapeDtypeStruct((b, t, ncols), F32),
        compiler_params=_params("parallel", "parallel"),
        name="in_proj",
    )(x, shift, scale, gain, w)


def _level_exponent(b, b_scr, logf, m, reverse):
    n = b.shape[0]
    if 2 * m >= 8:
        pieces = []
        for base in range(0, n, 2 * m):
            r = base + (m if reverse else m - 1)
            pieces.append(jnp.broadcast_to(b_scr[r:r + 1, :], (2 * m, b.shape[1])))
        return -jnp.abs(b - jnp.concatenate(pieces, axis=0))
    pos = lax.broadcasted_iota(jnp.int32, (n, 1), 0)
    if m == 1:
        sel = (pos % 2 == 0) if reverse else (pos % 2 == 1)
        return jnp.where(sel, logf, 0.0)
    assert m == 2
    r = pos % 4
    nxt = pltpu.roll(logf, n - 1, axis=0)
    prv = pltpu.roll(logf, 1, axis=0)
    if reverse:
        return jnp.where(r == 0, logf + nxt, jnp.where(r == 1, logf, jnp.where(r == 2, 0.0, prv)))
    return jnp.where(r == 0, nxt, jnp.where(r == 1, 0.0, jnp.where(r == 2, logf, logf + prv)))


def _scan_direction(f_logit, v, q_raw, lb, tri, lvl, st_ref, o_ref, b_scr, *, first, reverse):
    n = f_logit.shape[0]
    if first:
        e = jnp.exp(-jnp.abs(f_logit))
        logf = jnp.minimum(f_logit, 0.0) - jnp.log1p(e)
        kk = jnp.where(f_logit >= 0.0, e, 1.0) / (1.0 + e)
    else:
        gate = lb + (1.0 - lb) * jax.nn.sigmoid(f_logit)
        logf = jnp.log(gate)
        kk = 1.0 - gate
    hi, mid, lo = _split3(logf)
    b = _dot(tri, hi) + _dot(tri, mid) + _dot(tri, lo)
    last = b[0:1, :] if reverse else b[n - 1:n, :]
    k_end = (kk * jnp.exp(last - b)).astype(BF16)
    state_decay = jnp.exp(last)
    v16 = v.astype(BF16)

    if o_ref is not None:
        q = _silu(q_raw) * Q_SCALE
        q_in = (q * jnp.exp(b)).astype(BF16)
        b_scr[...] = b
        scores = [jnp.zeros((n, n), F32) for _ in range(HG_HEADS)]
        m, code = n // 2, n.bit_length() - 2
        while m >= 1:
            ex = jnp.exp(_level_exponent(b, b_scr, logf, m, reverse))
            q_m = (q * ex).astype(BF16)
            k_m = (kk * ex).astype(BF16)
            for h in range(HG_HEADS):
                hs = slice(h * HG_DIM, (h + 1) * HG_DIM)
                scores[h] = jnp.where(lvl == code, _dot_nt(q_m[:, hs], k_m[:, hs]), scores[h])
            m, code = m // 2, code - 1
        q16 = q.astype(BF16)
        k16 = kk.astype(BF16)
        diag_code = n.bit_length() - 1
        for h in range(HG_HEADS):
            hs = slice(h * HG_DIM, (h + 1) * HG_DIM)
            scores[h] = jnp.where(lvl == diag_code, _dot_nt(q16[:, hs], k16[:, hs]), scores[h])

    for h in range(HG_HEADS):
        hs = slice(h * HG_DIM, (h + 1) * HG_DIM)
        st = st_ref[0, h]
        if o_ref is not None:
            o_ref[0, :, hs] = _dot(scores[h].astype(BF16), v16[:, hs]) + _dot_nt(q_in[:, hs], st.astype(BF16))
        v_t = v[:, hs].T.astype(BF16)
        st_ref[0, h] = st * state_decay[:, hs] + _dot(v_t, k_end[:, hs])


def _lower_bound(logits_ref, layer):
    z = logits_ref[...]
    ez = jnp.exp(z - jnp.max(z, axis=0, keepdims=True))
    sm = ez / jnp.sum(ez, axis=0, keepdims=True)
    lb = jnp.zeros((1, z.shape[1]), F32)
    for j in range(1, layer + 1):
        lb = lb + sm[j:j + 1, :]
    return lb


def _scan_kernel(*refs, layer, want_o):
    if want_o:
        (ff_ref, fb_ref, vf_ref, vb_ref, qf_ref, qb_ref, lgf_ref, lgb_ref, trif_ref, trib_ref, lvlf_ref, lvlb_ref,
         h0f_ref, h0b_ref, of_ref, ob_ref, hf_ref, hb_ref, b_scr) = refs
    else:
        (ff_ref, fb_ref, vf_ref, vb_ref, lgf_ref, lgb_ref, trif_ref, trib_ref,
         h0f_ref, h0b_ref, hf_ref, hb_ref) = refs
        qf_ref = qb_ref = of_ref = ob_ref = lvlf_ref = lvlb_ref = b_scr = None

    @pl.when(pl.program_id(1) == 0)
    def _():
        hf_ref[...] = h0f_ref[...]
        hb_ref[...] = h0b_ref[...]

    first = layer == 0
    for reverse, f_ref, v_ref, q_ref, lg_ref, tri_ref, lvl_ref, st_ref, o_ref in (
            (False, ff_ref, vf_ref, qf_ref, lgf_ref, trif_ref, lvlf_ref, hf_ref, of_ref),
            (True, fb_ref, vb_ref, qb_ref, lgb_ref, trib_ref, lvlb_ref, hb_ref, ob_ref)):
        lb = None if first else _lower_bound(lg_ref, layer)
        _scan_direction(
            f_ref[0], v_ref[0], q_ref[0] if want_o else None, lb, tri_ref[...],
            lvl_ref[...] if want_o else None, st_ref, o_ref, b_scr, first=first, reverse=reverse)


def _scan_masks(n):
    t = jnp.arange(n, dtype=jnp.int32)[:, None]
    s = jnp.arange(n, dtype=jnp.int32)[None, :]
    x = t ^ s
    level = jnp.zeros((n, n), jnp.int32)
    for k in range(1, n.bit_length() - 1):
        level = jnp.where((x >> k) > 0, k, level)
    diag_code = n.bit_length() - 1
    lvl_f = jnp.where(t == s, diag_code, jnp.where(t > s, level, -1))
    return (t >= s).astype(BF16), (t <= s).astype(BF16), lvl_f, lvl_f.T


def _scan_call(p, lb_logits_f, lb_logits_b, h0f, h0b, layer, want_o):
    b, t, _ = p.shape
    nc = t // CHUNK
    w = HG_WIDTH
    tri_f, tri_b, lvl_f, lvl_b = _scan_masks(CHUNK)
    depth = lb_logits_f.shape[0]

    def col(block, rev):
        if rev:
            return pl.BlockSpec((1, CHUNK, w), lambda i, j: (i, nc - 1 - j, block))
        return pl.BlockSpec((1, CHUNK, w), lambda i, j: (i, j, block))

    state_spec = pl.BlockSpec((1, HG_HEADS, HG_DIM, HG_DIM), lambda i, j: (i, 0, 0, 0))
    state_shape = jax.ShapeDtypeStruct((b, HG_HEADS, HG_DIM, HG_DIM), F32)
    in_specs = [col(0, False), col(1, True), col(2, False), col(2, True)]
    args = [p, p, p, p]
    if want_o:
        in_specs += [col(3, False), col(3, True)]
        args += [p, p]
    in_specs += [_const_spec((depth, w)), _const_spec((depth, w)),
                 _const_spec((CHUNK, CHUNK)), _const_spec((CHUNK, CHUNK))]
    args += [lb_logits_f, lb_logits_b, tri_f, tri_b]
    if want_o:
        in_specs += [_const_spec((CHUNK, CHUNK)), _const_spec((CHUNK, CHUNK))]
        args += [lvl_f, lvl_b]
    in_specs += [state_spec, state_spec]
    args += [h0f, h0b]
    out_specs = [state_spec, state_spec]
    out_shape = [state_shape, state_shape]
    scratch = []
    if want_o:
        o_shape = jax.ShapeDtypeStruct((b, t, w), F32)
        out_specs = [pl.BlockSpec((1, CHUNK, w), lambda i, j: (i, j, 0)),
                     pl.BlockSpec((1, CHUNK, w), lambda i, j: (i, nc - 1 - j, 0))] + out_specs
        out_shape = [o_shape, o_shape] + out_shape
        scratch = [pltpu.VMEM((CHUNK, w), F32)]
    return pl.pallas_call(
        functools.partial(_scan_kernel, layer=layer, want_o=want_o),
        grid=(b, nc),
        in_specs=in_specs,
        out_specs=out_specs,
        out_shape=out_shape,
        scratch_shapes=scratch,
        compiler_params=_params("parallel", "arbitrary"),
        name="scan" if want_o else "scan_state",
    )(*args)


def _mix_out_kernel(of_ref, ob_ref, g_ref, u_ref, v_ref, x_ref, hgg_ref, lng_ref, lnb_ref, ws_ref, bs_ref,
                    wo_ref, g1_ref, sh_ref, sc_ref, ng_ref, xo_ref, h2_ref):
    tm = x_ref.shape[1]
    o = of_ref[0] + ob_ref[0]
    parts = []
    for h in range(HG_HEADS):
        oh = o[:, h * HG_DIM:(h + 1) * HG_DIM]
        parts.append(oh * lax.rsqrt(jnp.mean(oh * oh, axis=-1, keepdims=True) + EPS))
    o_hg = jnp.concatenate(parts, axis=-1) * hgg_ref[...] * _silu(g_ref[0])

    u = _gelu(u_ref[0])
    v = _gelu(v_ref[0])
    parts = []
    for h in range(MLP_HEADS):
        vh = v[:, h * MLP_DIM:(h + 1) * MLP_DIM]
        dv = vh - jnp.mean(vh, axis=-1, keepdims=True)
        parts.append(dv * lax.rsqrt(jnp.mean(dv * dv, axis=-1, keepdims=True) + EPS))
    vn = (jnp.concatenate(parts, axis=-1) * lng_ref[...] + lnb_ref[...]).astype(BF16)
    rows = []
    for c in range(tm // CHUNK):
        zc = [_dot(ws_ref[h], vn[c * CHUNK:(c + 1) * CHUNK, h * MLP_DIM:(h + 1) * MLP_DIM])
              for h in range(MLP_HEADS)]
        rows.append(jnp.concatenate(zc, axis=-1) + bs_ref[...])
    o_mlp = u * jnp.concatenate(rows, axis=0)

    y = _dot(o_hg.astype(BF16), wo_ref[0:HG_WIDTH, :]) + _dot(o_mlp.astype(BF16), wo_ref[HG_WIDTH:, :])
    xn = x_ref[0] + g1_ref[0] * y
    xo_ref[0] = xn
    ms = jnp.mean(xn * xn, axis=-1, keepdims=True)
    h2 = xn * lax.rsqrt(ms + EPS) * ng_ref[...]
    h2_ref[0] = (h2 * (1.0 + sc_ref[0]) + sh_ref[0]).astype(BF16)


def _mix_out_call(o_f, o_b, p, x, hg_gain, ln_g, ln_b, w_s, bias_full, w_out, gate1, shift2, scale2, norm_gain):
    b, t, d = x.shape
    tm = min(256, t)
    w = HG_WIDTH
    per_batch = gate1.shape[0] == b and b > 1
    mod_map = (lambda i, j: (i, 0, 0)) if per_batch else (lambda i, j: (0, 0, 0))
    tok = lambda i, j: (i, j, 0)
    return pl.pallas_call(
        _mix_out_kernel,
        grid=(b, t // tm),
        in_specs=[
            pl.BlockSpec((1, tm, w), tok),
            pl.BlockSpec((1, tm, w), tok),
            pl.BlockSpec((1, tm, w), lambda i, j: (i, j, 4)),
            pl.BlockSpec((1, tm, w), lambda i, j: (i, j, 5)),
            pl.BlockSpec((1, tm, w), lambda i, j: (i, j, 6)),
            pl.BlockSpec((1, tm, d), tok),
            _const_spec((1, w)), _const_spec((1, w)), _const_spec((1, w)),
            _const_spec((MLP_HEADS, CHUNK, CHUNK)),
            _const_spec((CHUNK, w)),
            _const_spec((d, d)),
            pl.BlockSpec((1, 1, d), mod_map), pl.BlockSpec((1, 1, d), mod_map), pl.BlockSpec((1, 1, d), mod_map),
            _const_spec((1, d)),
        ],
        out_specs=[pl.BlockSpec((1, tm, d), tok), pl.BlockSpec((1, tm, d), tok)],
        out_shape=[jax.ShapeDtypeStruct((b, t, d), F32), jax.ShapeDtypeStruct((b, t, d), BF16)],
        compiler_params=_params("parallel", "parallel"),
        name="mix_out",
    )(o_f, o_b, p, p, p, x, hg_gain, ln_g, ln_b, w_s, bias_full, w_out, gate1, shift2, scale2, norm_gain)


def _ffn_kernel(*refs, two_d, final):
    if two_d:
        h_ref, hp_ref, hn_ref = refs[:3]
        refs = refs[3:]
    else:
        h_ref = refs[0]
        refs = refs[1:]
    x_ref, g2_ref, wa_ref, wg_ref, cw_ref, cb_ref, wd_ref = refs[:7]
    refs = refs[7:]
    if final:
        fg_ref = refs[0]
        refs = refs[1:]
    o_ref, gext_ref, acc_ref = refs

    tm = x_ref.shape[1]
    nchunks, _, fc = wa_ref.shape
    hm = h_ref[0]
    pos = lax.broadcasted_iota(jnp.int32, (tm, 1), 0)
    if two_d:
        hp = hp_ref[0]
        hn = hn_ref[0]
        has_prev = pl.program_id(1) > 0
        has_next = pl.program_id(1) < pl.num_programs(1) - 1
        has_left = pos % GRID_W != 0
        has_right = pos % GRID_W != GRID_W - 1
        row_offsets = (0, GRID_W, 2 * GRID_W)
        ext = tm + 2 * GRID_W
    else:
        has_left = pos != 0
        has_right = pos != tm - 1
        row_offsets = (0,)
        ext = tm
    gext_ref[0:HALO_PAD, :] = jnp.zeros((HALO_PAD, fc), F32)
    gext_ref[HALO_PAD + ext:, :] = jnp.zeros((HALO_PAD, fc), F32)
    acc_ref[...] = jnp.zeros_like(acc_ref)

    def body(c, carry):
        a = _dot(hm, wa_ref[c])
        gm = _dot(hm, wg_ref[c])
        if two_d:
            gp = _dot(hp, wg_ref[c])
            gn = _dot(hn, wg_ref[c])
            gext_ref[HALO_PAD:HALO_PAD + GRID_W, :] = jnp.where(has_prev, gp, 0.0)
            gext_ref[HALO_PAD + GRID_W:HALO_PAD + GRID_W + tm, :] = gm
            gext_ref[HALO_PAD + GRID_W + tm:HALO_PAD + ext, :] = jnp.where(has_next, gn, 0.0)
        else:
            gext_ref[HALO_PAD:HALO_PAD + tm, :] = gm
        cw = cw_ref[c]
        conv = None
        for dc in (-1, 0, 1):
            part = None
            for ri, off in enumerate(row_offsets):
                start = HALO_PAD + off + dc
                tap = ri * 3 + dc + 1
                term = gext_ref[start:start + tm, :] * cw[tap:tap + 1, :]
                part = term if part is None else part + term
            if dc == -1:
                part = jnp.where(has_left, part, 0.0)
            elif dc == 1:
                part = jnp.where(has_right, part, 0.0)
            conv = part if conv is None else conv + part
        act = a * _gelu(conv + cb_ref[c])
        acc_ref[...] += _dot(act.astype(BF16), wd_ref[c])
        return carry

    lax.fori_loop(0, nchunks, body, 0)
    y = x_ref[0] + g2_ref[0] * acc_ref[...]
    if final:
        ms = jnp.mean(y * y, axis=-1, keepdims=True)
        y = y * lax.rsqrt(ms + EPS) * fg_ref[...]
    o_ref[0] = y


def _ffn_call(h2, x, gate2, w_a, w_g, conv_taps, conv_bias, w_d, two_d, final_gain=None):
    b, t, d = x.shape
    nchunks, _, fc = w_a.shape
    per_batch = gate2.shape[0] == b and b > 1
    mod_map = (lambda i, j: (i, 0, 0)) if per_batch else (lambda i, j: (0, 0, 0))
    tok = lambda i, j: (i, j, 0)
    if two_d:
        tm = min(512, t)
        rpt = tm // GRID_W
        nrows = t // GRID_W
        h_specs = [
            pl.BlockSpec((1, tm, d), tok),
            pl.BlockSpec((1, GRID_W, d), lambda i, j: (i, jnp.maximum(j * rpt - 1, 0), 0)),
            pl.BlockSpec((1, GRID_W, d), lambda i, j: (i, jnp.minimum((j + 1) * rpt, nrows - 1), 0)),
        ]
        h_args = [h2, h2, h2]
        ext = tm + 2 * GRID_W
    else:
        tm = t
        h_specs = [pl.BlockSpec((1, tm, d), tok)]
        h_args = [h2]
        ext = tm
    ntaps = conv_taps.shape[1]
    in_specs = h_specs + [
        pl.BlockSpec((1, tm, d), tok),
        pl.BlockSpec((1, 1, d), mod_map),
        _const_spec((nchunks, d, fc)), _const_spec((nchunks, d, fc)),
        _const_spec((nchunks, ntaps, fc)), _const_spec((nchunks, 1, fc)),
        _const_spec((nchunks, fc, d)),
    ]
    args = h_args + [x, gate2, w_a, w_g, conv_taps, conv_bias, w_d]
    if final_gain is not None:
        in_specs.append(_const_spec((1, d)))
        args.append(final_gain)
    return pl.pallas_call(
        functools.partial(_ffn_kernel, two_d=two_d, final=final_gain is not None),
        grid=(b, t // tm),
        in_specs=in_specs,
        out_specs=pl.BlockSpec((1, tm, d), tok),
        out_shape=jax.ShapeDtypeStruct((b, t, d), F32),
        scratch_shapes=[pltpu.VMEM((ext + 2 * HALO_PAD, fc), F32), pltpu.VMEM((tm, d), F32)],
        compiler_params=_params("parallel", "parallel"),
        name="ffn" if two_d else "ffn_ctx",
    )(*args)


def _chunked_cols(w, fc):
    d, f = w.shape
    return w.reshape(d, f // fc, fc).transpose(1, 0, 2)


def kernel(x, c, ctx, c_ctx, w_ada, b_ada, norm_mix, norm_ffn, w_in, lb_logits_fwd, lb_logits_bwd, hg_norm,
           sgu_norm_g, sgu_norm_b, w_spatial, b_spatial, w_out, w_up, conv_w, conv_b, w_down, norm_final):
    bsz, t, d = x.shape
    depth = w_ada.shape[0]
    f = w_down.shape[1]
    w = HG_WIDTH
    assert t % 512 == 0 and ctx.shape[1] % 256 == 0 and f % FFN_CHUNK == 0

    cond_rows = -(-(bsz + 1) // 8) * 8
    cond = jnp.concatenate([c, c_ctx[None, :], jnp.zeros((cond_rows - bsz - 1, d), F32)], axis=0)
    ada = _ada_call(cond, w_ada, b_ada)

    zeros_state = jnp.zeros((bsz, HG_HEADS, HG_DIM, HG_DIM), F32)
    xc = ctx
    for l in range(depth):
        last = l == depth - 1
        mods = ada[l].reshape(cond_rows, 6, d)
        sh1, sc1, g1, sh2, sc2, g2 = (mods[:bsz, k][:, None, :] for k in range(6))
        csh1, csc1, cg1, csh2, csc2, cg2 = (mods[bsz:bsz + 1, k][:, None, :] for k in range(6))
        w_in_l = w_in[l].astype(BF16)
        gain_mix = norm_mix[l][None, :]
        gain_ffn = norm_ffn[l][None, :]
        w_out_l = w_out[l].astype(BF16)
        w_a = _chunked_cols(w_up[l][:, :f].astype(BF16), FFN_CHUNK)
        w_g = _chunked_cols(w_up[l][:, f:].astype(BF16), FFN_CHUNK)
        w_d = w_down[l].astype(BF16).reshape(f // FFN_CHUNK, FFN_CHUNK, d)
        taps = _chunked_cols(conv_w[l].reshape(9, f), FFN_CHUNK)
        cbias = conv_b[l].reshape(f // FFN_CHUNK, 1, FFN_CHUNK)
        mix_w = (hg_norm[l][None, :], sgu_norm_g[l][None, :], sgu_norm_b[l][None, :], w_spatial[l].astype(BF16),
                 jnp.repeat(b_spatial[l].T, MLP_DIM, axis=1))

        if last:
            pc = _in_proj_call(xc, csh1, csc1, gain_mix, w_in_l, 3 * w)
            hf_c, hb_c = _scan_call(pc, lb_logits_fwd, lb_logits_bwd, zeros_state, zeros_state, l, False)
        else:
            pc = _in_proj_call(xc, csh1, csc1, gain_mix, w_in_l, w_in_l.shape[1])
            ocf, ocb, hf_c, hb_c = _scan_call(pc, lb_logits_fwd, lb_logits_bwd, zeros_state, zeros_state, l, True)

        p = _in_proj_call(x, sh1, sc1, gain_mix, w_in_l, w_in_l.shape[1])
        o_f, o_b, _, _ = _scan_call(p, lb_logits_fwd, lb_logits_bwd, hf_c, hb_c, l, True)
        x_mid, h2 = _mix_out_call(o_f, o_b, p, x, *mix_w, w_out_l, g1, sh2, sc2, gain_ffn)
        x = _ffn_call(h2, x_mid, g2, w_a, w_g, taps, cbias, w_d, True,
                      final_gain=norm_final[None, :] if last else None)

        if not last:
            xc_mid, hc2 = _mix_out_call(ocf, ocb, pc, xc, *mix_w, w_out_l, cg1, csh2, csc2, gain_ffn)
            xc = _ffn_call(hc2, xc_mid, cg2, w_a, w_g, taps[:, 3:6, :], cbias, w_d, False)
    return x
```

```python
import functools

import jax
import jax.numpy as jnp
from jax import lax
from jax.experimental import pallas as pl
from jax.experimental.pallas import tpu as pltpu

F32 = jnp.float32
BF16 = jnp.bfloat16

HG_DIM = 128
HG_HEADS = 4
HG_WIDTH = HG_DIM * HG_HEADS
MLP_HEADS = 4
MLP_DIM = 128
MLP_WIDTH = MLP_HEADS * MLP_DIM
GRID_W = 64
CHUNK = 128
EPS = 1e-6
Q_SCALE = HG_DIM ** -0.5
SQRT_HALF = 0.7071067811865476
LOG2_E = 1.4426950408889634

FFN_CHUNK = 256
FFN_ROWS = 1024
FFN_ROW_BLOCK = 128
FFN_MATMUL_ROWS = 256
SUBLANES = 8
VMEM_LIMIT = 56 * 1024 * 1024


def _const_spec(shape):
    zeros = (0,) * len(shape)
    return pl.BlockSpec(shape, lambda *_: zeros, pipeline_mode=pl.Buffered(1))


def _params(*semantics, flags=None):
    return pltpu.CompilerParams(dimension_semantics=semantics, vmem_limit_bytes=VMEM_LIMIT, flags=flags)


def _gelu(x):
    return 0.5 * x * (1.0 + lax.erf(x * SQRT_HALF))


def _silu(x):
    return x * jax.nn.sigmoid(x)


def _dot(a, b):
    return jnp.dot(a, b, preferred_element_type=F32)


def _dot_nt(a, b):
    return lax.dot_general(a, b, (((1,), (1,)), ((), ())), preferred_element_type=F32)


def _tile_op(op, x, tile):
    rows, n = x.shape
    return op(x.reshape(rows // SUBLANES, SUBLANES, n), tile[None]).reshape(rows, n)


def _split3(a):
    hi = a.astype(BF16)
    r = a - hi.astype(F32)
    mid = r.astype(BF16)
    lo = (r - mid.astype(F32)).astype(BF16)
    return hi, mid, lo


def _ada_kernel(c_ref, w_ref, b_ref, o_ref):
    a = _silu(c_ref[...])
    a_hi = a.astype(BF16)
    a_lo = (a - a_hi.astype(F32)).astype(BF16)
    w = w_ref[0]
    w_hi = w.astype(BF16)
    w_lo = (w - w_hi.astype(F32)).astype(BF16)
    o_ref[0] = _dot(a_hi, w_hi) + _dot(a_hi, w_lo) + _dot(a_lo, w_hi) + b_ref[0]


def _ada_call(cc, w_ada, b_ada):
    depth, d, e = w_ada.shape
    rows = cc.shape[0]
    tn = 1536
    return pl.pallas_call(
        _ada_kernel,
        grid=(depth, e // tn),
        in_specs=[
            pl.BlockSpec((rows, d), lambda l, j: (0, 0)),
            pl.BlockSpec((1, d, tn), lambda l, j: (l, 0, j)),
            pl.BlockSpec((1, 1, tn), lambda l, j: (l, 0, j)),
        ],
        out_specs=pl.BlockSpec((1, rows, tn), lambda l, j: (l, 0, j)),
        out_shape=jax.ShapeDtypeStruct((depth, rows, e), F32),
        compiler_params=_params("parallel", "parallel"),
        name="ada",
    )(cc, w_ada, b_ada.reshape(depth, 1, e))


def _in_proj_kernel(x_ref, sh_ref, sc_ref, gain_ref, w_ref, *o_refs, col_tile):
    x = x_ref[0]
    ms = jnp.mean(x * x, axis=-1, keepdims=True)
    h = x * lax.rsqrt(ms + EPS) * gain_ref[...]
    h = (h * (1.0 + sc_ref[0]) + sh_ref[0]).astype(BF16)
    col0 = 0
    for o_ref in o_refs:
        for j in range(o_ref.shape[2] // col_tile):
            cols = slice(j * col_tile, (j + 1) * col_tile)
            wcols = slice(col0 + j * col_tile, col0 + (j + 1) * col_tile)
            o_ref[0, :, cols] = _dot(h, w_ref[:, wcols]).astype(o_ref.dtype)
        col0 += o_ref.shape[2]


def _in_proj_call(x, shift, scale, gain, w, n_f32, n_bf16):
    b, t, d = x.shape
    tm = min(256, t)
    per_batch = shift.shape[0] == b and b > 1
    mod_map = (lambda i, j: (i, 0, 0)) if per_batch else (lambda i, j: (0, 0, 0))
    widths = [(n_f32, F32)] + ([(n_bf16, BF16)] if n_bf16 else [])
    return pl.pallas_call(
        functools.partial(_in_proj_kernel, col_tile=512),
        grid=(b, t // tm),
        in_specs=[
            pl.BlockSpec((1, tm, d), lambda i, j: (i, j, 0)),
            pl.BlockSpec((1, 1, d), mod_map),
            pl.BlockSpec((1, 1, d), mod_map),
            _const_spec((1, d)),
            pl.BlockSpec((d, n_f32 + n_bf16), lambda i, j: (0, 0), pipeline_mode=pl.Buffered(1)),
        ],
        out_specs=[pl.BlockSpec((1, tm, n), lambda i, j: (i, j, 0)) for n, _ in widths],
        out_shape=[jax.ShapeDtypeStruct((b, t, n), dt) for n, dt in widths],
        compiler_params=_params("parallel", "parallel"),
        name="in_proj",
    )(x, shift, scale, gain, w)


def _neg_abs(x):
    bits = lax.bitcast_convert_type(x, jnp.uint32) | jnp.uint32(0x80000000)
    return lax.bitcast_convert_type(bits, F32)


def _low_level_exponent(b2, logf2, m, reverse):
    n, w = b2.shape
    pos = lax.broadcasted_iota(jnp.int32, (n, 1), 0)
    if m == 1:
        sel = (pos % 2 == 0) if reverse else (pos % 2 == 1)
        return jnp.where(sel, logf2, 0.0)
    if m == 2:
        r = pos % 4
        nxt = pltpu.roll(logf2, n - 1, axis=0)
        prv = pltpu.roll(logf2, 1, axis=0)
        if reverse:
            return jnp.where(r == 0, logf2 + nxt, jnp.where(r == 1, logf2, jnp.where(r == 2, 0.0, prv)))
        return jnp.where(r == 0, nxt, jnp.where(r == 1, 0.0, jnp.where(r == 2, logf2, logf2 + prv)))
    assert m == 4
    b3 = b2.reshape(n // SUBLANES, SUBLANES, w)
    mid = m if reverse else m - 1
    return _neg_abs(b3 - b3[:, mid:mid + 1, :]).reshape(n, w)


def _scan_direction(f_logit, v, q_raw, lb, tri, lvl, st_ref, o_ref, row0, b_scr, *, first, reverse):
    n, w = f_logit.shape
    if first:
        e = jnp.exp(-jnp.abs(f_logit))
        logf = jnp.minimum(f_logit, 0.0) - jnp.log1p(e)
        kk = jnp.where(f_logit >= 0.0, e, 1.0) / (1.0 + e)
    else:
        gate = lb + (1.0 - lb) * jax.nn.sigmoid(f_logit)
        logf = jnp.log(gate)
        kk = 1.0 - gate
    logf2 = logf * LOG2_E
    hi = logf2.astype(BF16)
    lo = (logf2 - hi.astype(F32)).astype(BF16)
    b2 = _dot(tri, hi) + _dot(tri, lo)
    last2 = b2[0:1, :] if reverse else b2[n - 1:n, :]
    k16 = kk.astype(BF16)
    k_end = k16 * jnp.exp2(last2 - b2).astype(BF16)
    state_decay = jnp.exp2(last2)
    v16 = v.astype(BF16)

    if o_ref is not None:
        q = _silu(q_raw) * Q_SCALE
        q16 = q.astype(BF16)
        q_in = q16 * jnp.exp2(b2).astype(BF16)
        heads = [slice(h * HG_DIM, (h + 1) * HG_DIM) for h in range(HG_HEADS)]
        diag_code = n.bit_length() - 1
        on_diag = lvl == diag_code
        scores = [jnp.where(on_diag, _dot_nt(q16[:, hs], k16[:, hs]), 0.0) for hs in heads]

        m, code = 1, 0
        while 2 * m <= SUBLANES:
            ex = jnp.exp2(_low_level_exponent(b2, logf2, m, reverse)).astype(BF16)
            q_m = q16 * ex
            k_m = k16 * ex
            at_level = lvl == code
            for h, hs in enumerate(heads):
                scores[h] = jnp.where(at_level, _dot_nt(q_m[:, hs], k_m[:, hs]), scores[h])
            m, code = 2 * m, code + 1

        b_scr[...] = b2
        while m < n:
            bases = range(0, n, 2 * m)
            q_parts, k_parts = [], []
            for base in bases:
                first_half, second_half = slice(base, base + m), slice(base + m, base + 2 * m)
                t_rows, s_rows = (first_half, second_half) if reverse else (second_half, first_half)
                r = base + (m if reverse else m - 1)
                bref = b_scr[r:r + 1, :]
                q_parts.append(q[t_rows] * jnp.exp2(_neg_abs(b2[t_rows] - bref)))
                k_s = kk[s_rows] * jnp.exp2(_neg_abs(b2[s_rows] - bref))
                k_parts += [jnp.zeros_like(k_s), k_s] if reverse else [k_s, jnp.zeros_like(k_s)]
            at_level = lvl == code
            q_t = jnp.concatenate(q_parts, axis=0).astype(BF16)
            k_s = jnp.concatenate(k_parts, axis=0).astype(BF16)
            for h, hs in enumerate(heads):
                s_t = _dot_nt(q_t[:, hs], k_s[:, hs])
                rows = []
                for j, base in enumerate(bases):
                    first_half, second_half = slice(base, base + m), slice(base + m, base + 2 * m)
                    t_rows, s_rows = (first_half, second_half) if reverse else (second_half, first_half)
                    new = jnp.where(at_level[t_rows], s_t[j * m:(j + 1) * m], scores[h][t_rows])
                    rows += [new, scores[h][s_rows]] if reverse else [scores[h][s_rows], new]
                scores[h] = jnp.concatenate(rows, axis=0)
            m, code = 2 * m, code + 1

    for h in range(HG_HEADS):
        hs = slice(h * HG_DIM, (h + 1) * HG_DIM)
        st = st_ref[0, h]
        if o_ref is not None:
            o_ref[0, pl.ds(row0, n), hs] += (_dot(scores[h].astype(BF16), v16[:, hs])
                                             + _dot_nt(q_in[:, hs], st.astype(BF16)))
        v_t = v[:, hs].T.astype(BF16)
        st_ref[0, h] = st * state_decay[:, hs] + _dot(v_t, k_end[:, hs])


def _lower_bound(logits_ref, layer):
    z = logits_ref[...]
    ez = jnp.exp(z - jnp.max(z, axis=0, keepdims=True))
    sm = ez / jnp.sum(ez, axis=0, keepdims=True)
    lb = jnp.zeros((1, z.shape[1]), F32)
    for j in range(1, layer + 1):
        lb = lb + sm[j:j + 1, :]
    return lb


def _scan_kernel(*refs, layer, want_o):
    if want_o:
        (ff_ref, fb_ref, vf_ref, vb_ref, qf_ref, qb_ref, lgf_ref, lgb_ref, trif_ref, trib_ref, lvlf_ref, lvlb_ref,
         h0f_ref, h0b_ref, o_ref, hf_ref, hb_ref, b_scr) = refs
    else:
        (ff_ref, fb_ref, vf_ref, vb_ref, lgf_ref, lgb_ref, trif_ref, trib_ref,
         h0f_ref, h0b_ref, hf_ref, hb_ref) = refs
        qf_ref = qb_ref = o_ref = lvlf_ref = lvlb_ref = b_scr = None
    step = pl.program_id(1)

    @pl.when(step == 0)
    def _():
        hf_ref[...] = h0f_ref[...]
        hb_ref[...] = h0b_ref[...]
        if want_o:
            o_ref[...] = jnp.zeros_like(o_ref)

    first = layer == 0
    fwd_row0 = pl.multiple_of(step * CHUNK, CHUNK)
    bwd_row0 = pl.multiple_of((pl.num_programs(1) - 1 - step) * CHUNK, CHUNK)
    for reverse, f_ref, v_ref, q_ref, lg_ref, tri_ref, lvl_ref, st_ref, row0 in (
            (False, ff_ref, vf_ref, qf_ref, lgf_ref, trif_ref, lvlf_ref, hf_ref, fwd_row0),
            (True, fb_ref, vb_ref, qb_ref, lgb_ref, trib_ref, lvlb_ref, hb_ref, bwd_row0)):
        lb = None if first else _lower_bound(lg_ref, layer)
        _scan_direction(
            f_ref[0], v_ref[0], q_ref[0] if want_o else None, lb, tri_ref[...],
            lvl_ref[...] if want_o else None, st_ref, o_ref, row0, b_scr, first=first, reverse=reverse)


def _scan_masks(n):
    t = jnp.arange(n, dtype=jnp.int32)[:, None]
    s = jnp.arange(n, dtype=jnp.int32)[None, :]
    x = t ^ s
    level = jnp.zeros((n, n), jnp.int32)
    for k in range(1, n.bit_length() - 1):
        level = jnp.where((x >> k) > 0, k, level)
    diag_code = n.bit_length() - 1
    lvl_f = jnp.where(t == s, diag_code, jnp.where(t > s, level, -1))
    return (t >= s).astype(BF16), (t <= s).astype(BF16), lvl_f, lvl_f.T


def _scan_call(p, lb_logits_f, lb_logits_b, h0f, h0b, layer, want_o):
    b, t, _ = p.shape
    nc = t // CHUNK
    w = HG_WIDTH
    tri_f, tri_b, lvl_f, lvl_b = _scan_masks(CHUNK)
    depth = lb_logits_f.shape[0]

    def col(block, rev):
        if rev:
            return pl.BlockSpec((1, CHUNK, w), lambda i, j: (i, nc - 1 - j, block))
        return pl.BlockSpec((1, CHUNK, w), lambda i, j: (i, j, block))

    state_spec = pl.BlockSpec((1, HG_HEADS, HG_DIM, HG_DIM), lambda i, j: (i, 0, 0, 0))
    state_shape = jax.ShapeDtypeStruct((b, HG_HEADS, HG_DIM, HG_DIM), F32)
    in_specs = [col(0, False), col(1, True), col(2, False), col(2, True)]
    args = [p, p, p, p]
    if want_o:
        in_specs += [col(3, False), col(3, True)]
        args += [p, p]
    in_specs += [_const_spec((depth, w)), _const_spec((depth, w)),
                 _const_spec((CHUNK, CHUNK)), _const_spec((CHUNK, CHUNK))]
    args += [lb_logits_f, lb_logits_b, tri_f, tri_b]
    if want_o:
        in_specs += [_const_spec((CHUNK, CHUNK)), _const_spec((CHUNK, CHUNK))]
        args += [lvl_f, lvl_b]
    in_specs += [state_spec, state_spec]
    args += [h0f, h0b]
    out_specs = [state_spec, state_spec]
    out_shape = [state_shape, state_shape]
    scratch = []
    if want_o:
        out_specs = [pl.BlockSpec((1, t, w), lambda i, j: (i, 0, 0))] + out_specs
        out_shape = [jax.ShapeDtypeStruct((b, t, w), F32)] + out_shape
        scratch = [pltpu.VMEM((CHUNK, w), F32)]
    return pl.pallas_call(
        functools.partial(_scan_kernel, layer=layer, want_o=want_o),
        grid=(b, nc),
        in_specs=in_specs,
        out_specs=out_specs,
        out_shape=out_shape,
        scratch_shapes=scratch,
        compiler_params=_params("parallel", "arbitrary"),
        name="scan" if want_o else "scan_state",
    )(*args)


def _mix_out_kernel(o_ref, g_ref, u_ref, v_ref, x_ref, hgg_ref, lng_ref, lnb_ref, ws_ref, bs_ref,
                    wo_ref, g1_ref, sh_ref, sc_ref, ng_ref, xo_ref, h2_ref):
    tm = x_ref.shape[1]
    o = o_ref[0]
    parts = []
    for h in range(HG_HEADS):
        oh = o[:, h * HG_DIM:(h + 1) * HG_DIM]
        parts.append(oh * lax.rsqrt(jnp.mean(oh * oh, axis=-1, keepdims=True) + EPS))
    o_hg = jnp.concatenate(parts, axis=-1) * hgg_ref[...] * _silu(g_ref[0].astype(F32))

    u = _gelu(u_ref[0].astype(F32))
    v = _gelu(v_ref[0].astype(F32))
    parts = []
    for h in range(MLP_HEADS):
        vh = v[:, h * MLP_DIM:(h + 1) * MLP_DIM]
        dv = vh - jnp.mean(vh, axis=-1, keepdims=True)
        parts.append(dv * lax.rsqrt(jnp.mean(dv * dv, axis=-1, keepdims=True) + EPS))
    vn = (jnp.concatenate(parts, axis=-1) * lng_ref[...] + lnb_ref[...]).astype(BF16)
    rows = []
    for c in range(tm // CHUNK):
        zc = [_dot(ws_ref[h], vn[c * CHUNK:(c + 1) * CHUNK, h * MLP_DIM:(h + 1) * MLP_DIM])
              for h in range(MLP_HEADS)]
        rows.append(jnp.concatenate(zc, axis=-1) + bs_ref[...])
    o_mlp = u * jnp.concatenate(rows, axis=0)

    y = _dot(o_hg.astype(BF16), wo_ref[0:HG_WIDTH, :]) + _dot(o_mlp.astype(BF16), wo_ref[HG_WIDTH:, :])
    xn = x_ref[0] + g1_ref[0] * y
    xo_ref[0] = xn
    ms = jnp.mean(xn * xn, axis=-1, keepdims=True)
    h2 = xn * lax.rsqrt(ms + EPS) * ng_ref[...]
    h2_ref[0] = (h2 * (1.0 + sc_ref[0]) + sh_ref[0]).astype(BF16)


def _mix_out_call(o, p_b, x, hg_gain, ln_g, ln_b, w_s, bias_full, w_out, gate1, shift2, scale2, norm_gain):
    b, t, d = x.shape
    tm = min(256, t)
    w = HG_WIDTH
    per_batch = gate1.shape[0] == b and b > 1
    mod_map = (lambda i, j: (i, 0, 0)) if per_batch else (lambda i, j: (0, 0, 0))
    tok = lambda i, j: (i, j, 0)
    return pl.pallas_call(
        _mix_out_kernel,
        grid=(b, t // tm),
        in_specs=[
            pl.BlockSpec((1, tm, w), tok),
            pl.BlockSpec((1, tm, w), lambda i, j: (i, j, 0)),
            pl.BlockSpec((1, tm, w), lambda i, j: (i, j, 1)),
            pl.BlockSpec((1, tm, w), lambda i, j: (i, j, 2)),
            pl.BlockSpec((1, tm, d), tok),
            _const_spec((1, w)), _const_spec((1, w)), _const_spec((1, w)),
            _const_spec((MLP_HEADS, CHUNK, CHUNK)),
            _const_spec((CHUNK, w)),
            _const_spec((d, d)),
            pl.BlockSpec((1, 1, d), mod_map), pl.BlockSpec((1, 1, d), mod_map), pl.BlockSpec((1, 1, d), mod_map),
            _const_spec((1, d)),
        ],
        out_specs=[pl.BlockSpec((1, tm, d), tok), pl.BlockSpec((1, tm, d), tok)],
        out_shape=[jax.ShapeDtypeStruct((b, t, d), F32), jax.ShapeDtypeStruct((b, t, d), BF16)],
        compiler_params=_params("parallel", "parallel"),
        name="mix_out",
    )(o, p_b, p_b, p_b, x, hg_gain, ln_g, ln_b, w_s, bias_full, w_out, gate1, shift2, scale2, norm_gain)


def _ffn_kernel(*refs, two_d, final):
    if two_d:
        h_ref, hp_ref, hn_ref = refs[:3]
        refs = refs[3:]
    else:
        h_ref = refs[0]
        refs = refs[1:]
    x_ref, g2_ref, wag_ref, cw_ref, cb_ref, wd_ref = refs[:6]
    refs = refs[6:]
    if final:
        fg_ref = refs[0]
        refs = refs[1:]
    o_ref, a0, a1, g0, g1, acc_ref = refs
    a_bufs, g_bufs = (a0, a1), (g0, g1)

    tm = x_ref.shape[1]
    nchunks, fc, _ = wd_ref.shape
    mb = min(FFN_MATMUL_ROWS, tm)
    rb = min(FFN_ROW_BLOCK, tm)
    if two_d:
        has_prev = pl.program_id(1) > 0
        has_next = pl.program_id(1) < pl.num_programs(1) - 1
        row_offsets = (0, GRID_W, 2 * GRID_W)
        ext = tm + 2 * GRID_W
        main = SUBLANES + GRID_W
    else:
        row_offsets = (0,)
        ext = tm
        main = SUBLANES
    acc_ref[...] = jnp.zeros_like(acc_ref)
    for g_buf in g_bufs:
        g_buf[0:SUBLANES, :] = jnp.zeros((SUBLANES, fc), F32)
        g_buf[SUBLANES + ext:, :] = jnp.zeros((SUBLANES, fc), F32)

    def up_rows(c, slot, r0):
        g_buf = g_bufs[slot]
        h = h_ref[0, r0:r0 + mb, :]
        ag = _dot(h, wag_ref[c])
        a_bufs[slot][r0:r0 + mb, :] = ag[:, :fc]
        g_buf[main + r0:main + r0 + mb, :] = ag[:, fc:]
        if two_d and r0 == 0:
            g_buf[SUBLANES:main, :] = jnp.where(has_prev, _dot(hp_ref[0], wag_ref[c, :, fc:]), 0.0)
        if two_d and r0 + mb == tm:
            g_buf[main + tm:SUBLANES + ext, :] = jnp.where(has_next, _dot(hn_ref[0], wag_ref[c, :, fc:]), 0.0)

    def finish_rows(c, slot, r0):
        g_buf = g_bufs[slot]
        acts = []
        for q0 in range(r0, r0 + mb, rb):
            pos = q0 + lax.broadcasted_iota(jnp.int32, (rb, 1), 0)
            if two_d:
                has_left = pos % GRID_W != 0
                has_right = pos % GRID_W != GRID_W - 1
            else:
                has_left = pos != 0
                has_right = pos != tm - 1

            def taps(dc, lo, hi):
                part = None
                for ri, off in enumerate(row_offsets):
                    tap = ri * 3 + dc + 1
                    start = SUBLANES + off + q0
                    term = _tile_op(jnp.multiply, g_buf[start + lo:start + hi, :], cw_ref[c, tap])
                    part = term if part is None else part + term
                return part

            left = pltpu.roll(taps(-1, -SUBLANES, rb), 1, axis=0)[SUBLANES:]
            right = pltpu.roll(taps(1, 0, rb + SUBLANES), rb + SUBLANES - 1, axis=0)[:rb]
            conv = taps(0, 0, rb) + jnp.where(has_left, left, 0.0) + jnp.where(has_right, right, 0.0)
            act = a_bufs[slot][q0:q0 + rb, :] * _gelu(_tile_op(jnp.add, conv, cb_ref[c]))
            acts.append(act.astype(BF16))
        acc_ref[r0:r0 + mb, :] += _dot(jnp.concatenate(acts, axis=0), wd_ref[c])

    def step(up, fin):
        for r0 in range(0, tm, mb):
            if up is not None:
                up_rows(*up, r0)
            if fin is not None:
                finish_rows(*fin, r0)

    step((0, 0), None)
    npairs = (nchunks - 1) // 2

    def pair(j, carry):
        c = 2 * j
        step((c + 1, 1), (c, 0))
        step((c + 2, 0), (c + 1, 1))
        return carry

    lax.fori_loop(0, npairs, pair, 0)
    if nchunks % 2 == 0:
        step((nchunks - 1, 1), (nchunks - 2, 0))
        step(None, (nchunks - 1, 1))
    else:
        step(None, (nchunks - 1, 0))
    y = x_ref[0] + g2_ref[0] * acc_ref[...]
    if final:
        ms = jnp.mean(y * y, axis=-1, keepdims=True)
        y = y * lax.rsqrt(ms + EPS) * fg_ref[...]
    o_ref[0] = y


def _ffn_call(h2, x, gate2, w_ag, conv_taps, conv_bias, w_d, two_d, final_gain=None):
    b, t, d = x.shape
    nchunks, fc, _ = w_d.shape
    per_batch = gate2.shape[0] == b and b > 1
    mod_map = (lambda i, j: (i, 0, 0)) if per_batch else (lambda i, j: (0, 0, 0))
    tok = lambda i, j: (i, j, 0)
    if two_d:
        tm = min(FFN_ROWS, t)
        rpt = tm // GRID_W
        nrows = t // GRID_W
        h_specs = [
            pl.BlockSpec((1, tm, d), tok),
            pl.BlockSpec((1, GRID_W, d), lambda i, j: (i, jnp.maximum(j * rpt - 1, 0), 0)),
            pl.BlockSpec((1, GRID_W, d), lambda i, j: (i, jnp.minimum((j + 1) * rpt, nrows - 1), 0)),
        ]
        h_args = [h2, h2, h2]
        ext = tm + 2 * GRID_W
    else:
        tm = t
        h_specs = [pl.BlockSpec((1, tm, d), tok)]
        h_args = [h2]
        ext = tm
    ntaps = conv_taps.shape[1]
    in_specs = h_specs + [
        pl.BlockSpec((1, tm, d), tok),
        pl.BlockSpec((1, 1, d), mod_map),
        _const_spec((nchunks, d, 2 * fc)),
        _const_spec((nchunks, ntaps, SUBLANES, fc)), _const_spec((nchunks, SUBLANES, fc)),
        _const_spec((nchunks, fc, d)),
    ]
    args = h_args + [x, gate2, w_ag, conv_taps, conv_bias, w_d]
    if final_gain is not None:
        in_specs.append(_const_spec((1, d)))
        args.append(final_gain)
    return pl.pallas_call(
        functools.partial(_ffn_kernel, two_d=two_d, final=final_gain is not None),
        grid=(b, t // tm),
        in_specs=in_specs,
        out_specs=pl.BlockSpec((1, tm, d), tok),
        out_shape=jax.ShapeDtypeStruct((b, t, d), F32),
        scratch_shapes=[pltpu.VMEM((tm, fc), F32), pltpu.VMEM((tm, fc), F32),
                        pltpu.VMEM((ext + 2 * SUBLANES, fc), F32), pltpu.VMEM((ext + 2 * SUBLANES, fc), F32),
                        pltpu.VMEM((tm, d), F32)],
        compiler_params=_params("parallel", "parallel"),
        name="ffn" if two_d else "ffn_ctx",
    )(*args)


def _sublane_tiles(a):
    return jnp.broadcast_to(a[..., None, :], a.shape[:-1] + (SUBLANES, a.shape[-1]))


def _chunked_cols(w, fc):
    d, f = w.shape
    return w.reshape(d, f // fc, fc).transpose(1, 0, 2)


def kernel(x, c, ctx, c_ctx, w_ada, b_ada, norm_mix, norm_ffn, w_in, lb_logits_fwd, lb_logits_bwd, hg_norm,
           sgu_norm_g, sgu_norm_b, w_spatial, b_spatial, w_out, w_up, conv_w, conv_b, w_down, norm_final):
    bsz, t, d = x.shape
    depth = w_ada.shape[0]
    f = w_down.shape[1]
    w = HG_WIDTH
    assert t % 512 == 0 and ctx.shape[1] % 256 == 0 and f % FFN_CHUNK == 0

    cond_rows = -(-(bsz + 1) // 8) * 8
    cond = jnp.concatenate([c, c_ctx[None, :], jnp.zeros((cond_rows - bsz - 1, d), F32)], axis=0)
    ada = _ada_call(cond, w_ada, b_ada)

    zeros_state = jnp.zeros((bsz, HG_HEADS, HG_DIM, HG_DIM), F32)
    xc = ctx
    for l in range(depth):
        last = l == depth - 1
        mods = ada[l].reshape(cond_rows, 6, d)
        sh1, sc1, g1, sh2, sc2, g2 = (mods[:bsz, k][:, None, :] for k in range(6))
        csh1, csc1, cg1, csh2, csc2, cg2 = (mods[bsz:bsz + 1, k][:, None, :] for k in range(6))
        w_in_l = w_in[l].astype(BF16)
        gain_mix = norm_mix[l][None, :]
        gain_ffn = norm_ffn[l][None, :]
        w_out_l = w_out[l].astype(BF16)
        w_up_l = w_up[l].astype(BF16)
        w_ag = jnp.concatenate([_chunked_cols(w_up_l[:, :f], FFN_CHUNK), _chunked_cols(w_up_l[:, f:], FFN_CHUNK)],
                               axis=2)
        w_d = w_down[l].astype(BF16).reshape(f // FFN_CHUNK, FFN_CHUNK, d)
        taps = _sublane_tiles(_chunked_cols(conv_w[l].reshape(9, f), FFN_CHUNK))
        cbias = _sublane_tiles(conv_b[l].reshape(f // FFN_CHUNK, FFN_CHUNK))
        mix_w = (hg_norm[l][None, :], sgu_norm_g[l][None, :], sgu_norm_b[l][None, :], w_spatial[l].astype(BF16),
                 jnp.repeat(b_spatial[l].T, MLP_DIM, axis=1))

        if last:
            pc, = _in_proj_call(xc, csh1, csc1, gain_mix, w_in_l, 3 * w, 0)
            hf_c, hb_c = _scan_call(pc, lb_logits_fwd, lb_logits_bwd, zeros_state, zeros_state, l, False)
        else:
            pc, pc_b = _in_proj_call(xc, csh1, csc1, gain_mix, w_in_l, 4 * w, 3 * w)
            oc, hf_c, hb_c = _scan_call(pc, lb_logits_fwd, lb_logits_bwd, zeros_state, zeros_state, l, True)

        p, p_b = _in_proj_call(x, sh1, sc1, gain_mix, w_in_l, 4 * w, 3 * w)
        o, _, _ = _scan_call(p, lb_logits_fwd, lb_logits_bwd, hf_c, hb_c, l, True)
        x_mid, h2 = _mix_out_call(o, p_b, x, *mix_w, w_out_l, g1, sh2, sc2, gain_ffn)
        x = _ffn_call(h2, x_mid, g2, w_ag, taps, cbias, w_d, True,
                      final_gain=norm_final[None, :] if last else None)

        if not last:
            xc_mid, hc2 = _mix_out_call(oc, pc_b, xc, *mix_w, w_out_l, cg1, csh2, csc2, gain_ffn)
            xc = _ffn_call(hc2, xc_mid, cg2, w_ag, taps[:, 3:6], cbias, w_d, False)
    return x
```

```python
import functools

import jax
import jax.numpy as jnp
from jax import lax
from jax.experimental import pallas as pl
from jax.experimental.pallas import tpu as pltpu

F32 = jnp.float32
BF16 = jnp.bfloat16

HG_DIM = 128
HG_HEADS = 4
HG_WIDTH = HG_DIM * HG_HEADS
MLP_HEADS = 4
MLP_DIM = 128
MLP_WIDTH = MLP_HEADS * MLP_DIM
GRID_W = 64
CHUNK = 128
EPS = 1e-6
Q_SCALE = HG_DIM ** -0.5
SQRT_HALF = 0.7071067811865476
LOG2_E = 1.4426950408889634

IN_PROJ_ROWS = 256
IN_PROJ_COLS = 512
FFN_CHUNK = 256
FFN_ROWS = 1024
FFN_ROW_BLOCK = 128
FFN_MATMUL_ROWS = 256
MIX_ROWS = 1024
MIX_SUB_ROWS = 256
SUBLANES = 8
VMEM_LIMIT = 56 * 1024 * 1024


def _const_spec(shape):
    zeros = (0,) * len(shape)
    return pl.BlockSpec(shape, lambda *_: zeros, pipeline_mode=pl.Buffered(1))


def _params(*semantics):
    return pltpu.CompilerParams(dimension_semantics=semantics, vmem_limit_bytes=VMEM_LIMIT)


def _mod_map(mod, batch):
    if mod.shape[0] == batch and batch > 1:
        return lambda i, j: (i, 0, 0)
    return lambda i, j: (0, 0, 0)


def _gelu(x):
    return 0.5 * x * (1.0 + lax.erf(x * SQRT_HALF))


def _silu(x):
    return x * jax.nn.sigmoid(x)


def _dot(a, b):
    return jnp.dot(a, b, preferred_element_type=F32)


def _dot_nt(a, b):
    return lax.dot_general(a, b, (((1,), (1,)), ((), ())), preferred_element_type=F32)


def _tile_op(op, x, tile):
    rows, n = x.shape
    return op(x.reshape(rows // SUBLANES, SUBLANES, n), tile[None]).reshape(rows, n)


def _normed(x, gain, shift, scale):
    ms = jnp.mean(x * x, axis=-1, keepdims=True)
    h = x * lax.rsqrt(ms + EPS) * gain
    return (h * (1.0 + scale) + shift).astype(BF16)


def _ada_kernel(c_ref, w_ref, b_ref, o_ref):
    a = _silu(c_ref[...])
    a_hi = a.astype(BF16)
    a_lo = (a - a_hi.astype(F32)).astype(BF16)
    w = w_ref[0]
    w_hi = w.astype(BF16)
    w_lo = (w - w_hi.astype(F32)).astype(BF16)
    o_ref[0] = _dot(a_hi, w_hi) + _dot(a_hi, w_lo) + _dot(a_lo, w_hi) + b_ref[0]


def _ada_call(cc, w_ada, b_ada):
    depth, d, e = w_ada.shape
    rows = cc.shape[0]
    tn = 1536
    return pl.pallas_call(
        _ada_kernel,
        grid=(depth, e // tn),
        in_specs=[
            pl.BlockSpec((rows, d), lambda l, j: (0, 0)),
            pl.BlockSpec((1, d, tn), lambda l, j: (l, 0, j)),
            pl.BlockSpec((1, 1, tn), lambda l, j: (l, 0, j)),
        ],
        out_specs=pl.BlockSpec((1, rows, tn), lambda l, j: (l, 0, j)),
        out_shape=jax.ShapeDtypeStruct((depth, rows, e), F32),
        compiler_params=_params("parallel", "parallel"),
        name="ada",
    )(cc, w_ada, b_ada.reshape(depth, 1, e))


def _in_proj_kernel(x_ref, sh_ref, sc_ref, gain_ref, w_ref, o_ref):
    h = _normed(x_ref[0], gain_ref[...], sh_ref[0], sc_ref[0])
    for c0 in range(0, o_ref.shape[2], IN_PROJ_COLS):
        cols = slice(c0, c0 + IN_PROJ_COLS)
        o_ref[0, :, cols] = _dot(h, w_ref[:, cols])


def _in_proj_call(x, shift, scale, gain, w, ncols):
    b, t, d = x.shape
    tm = min(IN_PROJ_ROWS, t)
    return pl.pallas_call(
        _in_proj_kernel,
        grid=(b, t // tm),
        in_specs=[
            pl.BlockSpec((1, tm, d), lambda i, j: (i, j, 0)),
            pl.BlockSpec((1, 1, d), _mod_map(shift, b)),
            pl.BlockSpec((1, 1, d), _mod_map(scale, b)),
            _const_spec((1, d)),
            pl.BlockSpec((d, ncols), lambda i, j: (0, 0), pipeline_mode=pl.Buffered(1)),
        ],
        out_specs=pl.BlockSpec((1, tm, ncols), lambda i, j: (i, j, 0)),
        out_shape=jax.ShapeDtypeStruct((b, t, ncols), F32),
        compiler_params=_params("parallel", "parallel"),
        name="in_proj",
    )(x, shift, scale, gain, w)


def _low_level_exponent(b2, logf2, m, reverse):
    n, w = b2.shape
    pos = lax.broadcasted_iota(jnp.int32, (n, 1), 0)
    if m == 1:
        sel = (pos % 2 == 0) if reverse else (pos % 2 == 1)
        return jnp.where(sel, logf2, 0.0)
    if m == 2:
        r = pos % 4
        nxt = pltpu.roll(logf2, n - 1, axis=0)
        prv = pltpu.roll(logf2, 1, axis=0)
        if reverse:
            return jnp.where(r == 0, logf2 + nxt, jnp.where(r == 1, logf2, jnp.where(r == 2, 0.0, prv)))
        return jnp.where(r == 0, nxt, jnp.where(r == 1, 0.0, jnp.where(r == 2, logf2, logf2 + prv)))
    assert m == 4
    b3 = b2.reshape(n // SUBLANES, SUBLANES, w)
    mid = m if reverse else m - 1
    return (-jnp.abs(b3 - b3[:, mid:mid + 1, :])).reshape(n, w)


def _scan_direction(f_logit, v, q_raw, lb, tri, lvl, st_ref, o_ref, row0, b_scr, *, first, reverse):
    n, w = f_logit.shape
    if first:
        e = jnp.exp(-jnp.abs(f_logit))
        logf = jnp.minimum(f_logit, 0.0) - jnp.log1p(e)
        kk = jnp.where(f_logit >= 0.0, e, 1.0) / (1.0 + e)
    else:
        gate = lb + (1.0 - lb) * jax.nn.sigmoid(f_logit)
        logf = jnp.log(gate)
        kk = 1.0 - gate
    logf2 = logf * LOG2_E
    hi = logf2.astype(BF16)
    lo = (logf2 - hi.astype(F32)).astype(BF16)
    b2 = _dot(tri, hi) + _dot(tri, lo)
    last2 = b2[0:1, :] if reverse else b2[n - 1:n, :]
    k16 = kk.astype(BF16)
    k_end = k16 * jnp.exp2(last2 - b2).astype(BF16)
    state_decay = jnp.exp2(last2)
    v16 = v.astype(BF16)

    if o_ref is not None:
        q = _silu(q_raw) * Q_SCALE
        q16 = q.astype(BF16)
        q_in = q16 * jnp.exp2(b2).astype(BF16)
        heads = [slice(h * HG_DIM, (h + 1) * HG_DIM) for h in range(HG_HEADS)]
        diag_code = n.bit_length() - 1
        on_diag = lvl == diag_code
        scores = [jnp.where(on_diag, _dot_nt(q16[:, hs], k16[:, hs]), 0.0) for hs in heads]

        m, code = 1, 0
        while 2 * m <= SUBLANES:
            ex = jnp.exp2(_low_level_exponent(b2, logf2, m, reverse)).astype(BF16)
            q_m = q16 * ex
            k_m = k16 * ex
            at_level = lvl == code
            for h, hs in enumerate(heads):
                scores[h] = jnp.where(at_level, _dot_nt(q_m[:, hs], k_m[:, hs]), scores[h])
            m, code = 2 * m, code + 1

        b_scr[...] = b2
        while m < n:
            bases = range(0, n, 2 * m)
            q_parts, k_parts = [], []
            for base in bases:
                first_half, second_half = slice(base, base + m), slice(base + m, base + 2 * m)
                t_rows, s_rows = (first_half, second_half) if reverse else (second_half, first_half)
                r = base + (m if reverse else m - 1)
                bref = b_scr[r:r + 1, :]
                q_parts.append(q[t_rows] * jnp.exp2(b2[t_rows] - bref))
                k_s = kk[s_rows] * jnp.exp2(bref - b2[s_rows])
                k_parts += [jnp.zeros_like(k_s), k_s] if reverse else [k_s, jnp.zeros_like(k_s)]
            at_level = lvl == code
            q_t = jnp.concatenate(q_parts, axis=0).astype(BF16)
            k_s = jnp.concatenate(k_parts, axis=0).astype(BF16)
            for h, hs in enumerate(heads):
                s_t = _dot_nt(q_t[:, hs], k_s[:, hs])
                rows = []
                for j, base in enumerate(bases):
                    first_half, second_half = slice(base, base + m), slice(base + m, base + 2 * m)
                    t_rows, s_rows = (first_half, second_half) if reverse else (second_half, first_half)
                    new = jnp.where(at_level[t_rows], s_t[j * m:(j + 1) * m], scores[h][t_rows])
                    rows += [new, scores[h][s_rows]] if reverse else [scores[h][s_rows], new]
                scores[h] = jnp.concatenate(rows, axis=0)
            m, code = 2 * m, code + 1

    for h in range(HG_HEADS):
        hs = slice(h * HG_DIM, (h + 1) * HG_DIM)
        st = st_ref[0, h]
        if o_ref is not None:
            o_ref[0, pl.ds(row0, n), hs] += (_dot(scores[h].astype(BF16), v16[:, hs])
                                             + _dot_nt(q_in[:, hs], st.astype(BF16)))
        v_t = v[:, hs].T.astype(BF16)
        st_ref[0, h] = st * state_decay[:, hs] + _dot(v_t, k_end[:, hs])


def _lower_bound(logits_ref, layer):
    z = logits_ref[...]
    ez = jnp.exp(z - jnp.max(z, axis=0, keepdims=True))
    sm = ez / jnp.sum(ez, axis=0, keepdims=True)
    lb = jnp.zeros((1, z.shape[1]), F32)
    for j in range(1, layer + 1):
        lb = lb + sm[j:j + 1, :]
    return lb


def _scan_kernel(*refs, layer, want_o):
    if want_o:
        (ff_ref, fb_ref, vf_ref, vb_ref, qf_ref, qb_ref, lgf_ref, lgb_ref, trif_ref, trib_ref, lvlf_ref, lvlb_ref,
         h0f_ref, h0b_ref, o_ref, hf_ref, hb_ref, b_scr) = refs
    else:
        (ff_ref, fb_ref, vf_ref, vb_ref, lgf_ref, lgb_ref, trif_ref, trib_ref,
         h0f_ref, h0b_ref, hf_ref, hb_ref) = refs
        qf_ref = qb_ref = o_ref = lvlf_ref = lvlb_ref = b_scr = None
    step = pl.program_id(1)

    @pl.when(step == 0)
    def _():
        hf_ref[...] = h0f_ref[...]
        hb_ref[...] = h0b_ref[...]
        if want_o:
            o_ref[...] = jnp.zeros_like(o_ref)

    first = layer == 0
    fwd_row0 = pl.multiple_of(step * CHUNK, CHUNK)
    bwd_row0 = pl.multiple_of((pl.num_programs(1) - 1 - step) * CHUNK, CHUNK)
    for reverse, f_ref, v_ref, q_ref, lg_ref, tri_ref, lvl_ref, st_ref, row0 in (
            (False, ff_ref, vf_ref, qf_ref, lgf_ref, trif_ref, lvlf_ref, hf_ref, fwd_row0),
            (True, fb_ref, vb_ref, qb_ref, lgb_ref, trib_ref, lvlb_ref, hb_ref, bwd_row0)):
        lb = None if first else _lower_bound(lg_ref, layer)
        _scan_direction(
            f_ref[0], v_ref[0], q_ref[0] if want_o else None, lb, tri_ref[...],
            lvl_ref[...] if want_o else None, st_ref, o_ref, row0, b_scr, first=first, reverse=reverse)


def _scan_masks(n):
    t = jnp.arange(n, dtype=jnp.int32)[:, None]
    s = jnp.arange(n, dtype=jnp.int32)[None, :]
    x = t ^ s
    level = jnp.zeros((n, n), jnp.int32)
    for k in range(1, n.bit_length() - 1):
        level = jnp.where((x >> k) > 0, k, level)
    diag_code = n.bit_length() - 1
    lvl_f = jnp.where(t == s, diag_code, jnp.where(t > s, level, -1))
    return (t >= s).astype(BF16), (t <= s).astype(BF16), lvl_f, lvl_f.T


def _scan_call(p, lb_logits_f, lb_logits_b, h0f, h0b, layer, want_o):
    b, t, _ = p.shape
    nc = t // CHUNK
    w = HG_WIDTH
    tri_f, tri_b, lvl_f, lvl_b = _scan_masks(CHUNK)
    depth = lb_logits_f.shape[0]

    def col(block, rev):
        if rev:
            return pl.BlockSpec((1, CHUNK, w), lambda i, j: (i, nc - 1 - j, block))
        return pl.BlockSpec((1, CHUNK, w), lambda i, j: (i, j, block))

    state_spec = pl.BlockSpec((1, HG_HEADS, HG_DIM, HG_DIM), lambda i, j: (i, 0, 0, 0))
    state_shape = jax.ShapeDtypeStruct((b, HG_HEADS, HG_DIM, HG_DIM), F32)
    in_specs = [col(0, False), col(1, True), col(2, False), col(2, True)]
    args = [p, p, p, p]
    if want_o:
        in_specs += [col(3, False), col(3, True)]
        args += [p, p]
    in_specs += [_const_spec((depth, w)), _const_spec((depth, w)),
                 _const_spec((CHUNK, CHUNK)), _const_spec((CHUNK, CHUNK))]
    args += [lb_logits_f, lb_logits_b, tri_f, tri_b]
    if want_o:
        in_specs += [_const_spec((CHUNK, CHUNK)), _const_spec((CHUNK, CHUNK))]
        args += [lvl_f, lvl_b]
    in_specs += [state_spec, state_spec]
    args += [h0f, h0b]
    out_specs = [state_spec, state_spec]
    out_shape = [state_shape, state_shape]
    scratch = []
    if want_o:
        out_specs = [pl.BlockSpec((1, t, w), lambda i, j: (i, 0, 0))] + out_specs
        out_shape = [jax.ShapeDtypeStruct((b, t, w), F32)] + out_shape
        scratch = [pltpu.VMEM((CHUNK, w), F32)]
    return pl.pallas_call(
        functools.partial(_scan_kernel, layer=layer, want_o=want_o),
        grid=(b, nc),
        in_specs=in_specs,
        out_specs=out_specs,
        out_shape=out_shape,
        scratch_shapes=scratch,
        compiler_params=_params("parallel", "arbitrary"),
        name="scan" if want_o else "scan_state",
    )(*args)


def _mix_out_kernel(o_ref, x_ref, sh1_ref, sc1_ref, ng1_ref, wguv_ref, hgg_ref, lng_ref, lnb_ref, ws_ref, bs_ref,
                    wo_ref, g1_ref, sh2_ref, sc2_ref, ng2_ref, xo_ref, h2_ref):
    tm = x_ref.shape[1]
    w = HG_WIDTH
    sub = min(MIX_SUB_ROWS, tm)

    def project(r0):
        h = _normed(x_ref[0, r0:r0 + sub, :], ng1_ref[...], sh1_ref[0], sc1_ref[0])
        return _dot(h, wguv_ref[...])

    guv_next = project(0)
    for r0 in range(0, tm, sub):
        blk = slice(r0, r0 + sub)
        guv = guv_next
        if r0 + sub < tm:
            guv_next = project(r0 + sub)
        o = o_ref[0, blk, :]
        parts = []
        for h in range(HG_HEADS):
            oh = o[:, h * HG_DIM:(h + 1) * HG_DIM]
            parts.append(oh * lax.rsqrt(jnp.mean(oh * oh, axis=-1, keepdims=True) + EPS))
        o_hg = jnp.concatenate(parts, axis=-1) * hgg_ref[...] * _silu(guv[:, :w])

        u = _gelu(guv[:, w:w + MLP_WIDTH])
        v = _gelu(guv[:, w + MLP_WIDTH:])
        parts = []
        for h in range(MLP_HEADS):
            vh = v[:, h * MLP_DIM:(h + 1) * MLP_DIM]
            dv = vh - jnp.mean(vh, axis=-1, keepdims=True)
            parts.append(dv * lax.rsqrt(jnp.mean(dv * dv, axis=-1, keepdims=True) + EPS))
        vn = (jnp.concatenate(parts, axis=-1) * lng_ref[...] + lnb_ref[...]).astype(BF16)
        rows = []
        for c in range(sub // CHUNK):
            zc = [_dot(ws_ref[h], vn[c * CHUNK:(c + 1) * CHUNK, h * MLP_DIM:(h + 1) * MLP_DIM])
                  for h in range(MLP_HEADS)]
            rows.append(jnp.concatenate(zc, axis=-1) + bs_ref[...])
        o_mlp = u * jnp.concatenate(rows, axis=0)

        y = _dot(o_hg.astype(BF16), wo_ref[0:HG_WIDTH, :]) + _dot(o_mlp.astype(BF16), wo_ref[HG_WIDTH:, :])
        xn = x_ref[0, blk, :] + g1_ref[0] * y
        xo_ref[0, blk, :] = xn
        h2_ref[0, blk, :] = _normed(xn, ng2_ref[...], sh2_ref[0], sc2_ref[0])


def _mix_out_call(o, x, shift1, scale1, gain1, w_guv, hg_gain, ln_g, ln_b, w_s, bias_full, w_out,
                  gate1, shift2, scale2, gain2):
    b, t, d = x.shape
    tm = min(MIX_ROWS, t)
    w = HG_WIDTH
    tok = lambda i, j: (i, j, 0)
    mod = lambda m: pl.BlockSpec((1, 1, d), _mod_map(m, b))
    return pl.pallas_call(
        _mix_out_kernel,
        grid=(b, t // tm),
        in_specs=[
            pl.BlockSpec((1, tm, w), tok),
            pl.BlockSpec((1, tm, d), tok),
            mod(shift1), mod(scale1), _const_spec((1, d)),
            _const_spec(w_guv.shape),
            _const_spec((1, w)), _const_spec((1, w)), _const_spec((1, w)),
            _const_spec((MLP_HEADS, CHUNK, CHUNK)),
            _const_spec((CHUNK, w)),
            _const_spec((d, d)),
            mod(gate1), mod(shift2), mod(scale2),
            _const_spec((1, d)),
        ],
        out_specs=[pl.BlockSpec((1, tm, d), tok), pl.BlockSpec((1, tm, d), tok)],
        out_shape=[jax.ShapeDtypeStruct((b, t, d), F32), jax.ShapeDtypeStruct((b, t, d), BF16)],
        compiler_params=_params("parallel", "parallel"),
        name="mix_out",
    )(o, x, shift1, scale1, gain1, w_guv, hg_gain, ln_g, ln_b, w_s, bias_full, w_out, gate1, shift2, scale2, gain2)


def _ffn_kernel(*refs, two_d, final):
    if two_d:
        h_ref, hp_ref, hn_ref = refs[:3]
        refs = refs[3:]
    else:
        h_ref = refs[0]
        refs = refs[1:]
    x_ref, g2_ref, wag_ref, cw_ref, cb_ref, wd_ref = refs[:6]
    refs = refs[6:]
    if final:
        fg_ref = refs[0]
        refs = refs[1:]
    o_ref, a0, a1, g0, g1, acc_ref = refs
    a_bufs, g_bufs = (a0, a1), (g0, g1)

    tm = x_ref.shape[1]
    nchunks, fc, _ = wd_ref.shape
    mb = min(FFN_MATMUL_ROWS, tm)
    rb = min(FFN_ROW_BLOCK, tm)
    if two_d:
        has_prev = pl.program_id(1) > 0
        has_next = pl.program_id(1) < pl.num_programs(1) - 1
        row_offsets = (0, GRID_W, 2 * GRID_W)
        ext = tm + 2 * GRID_W
        main = SUBLANES + GRID_W
    else:
        row_offsets = (0,)
        ext = tm
        main = SUBLANES
    acc_ref[...] = jnp.zeros_like(acc_ref)
    for g_buf in g_bufs:
        g_buf[0:SUBLANES, :] = jnp.zeros((SUBLANES, fc), F32)
        g_buf[SUBLANES + ext:, :] = jnp.zeros((SUBLANES, fc), F32)

    def up_rows(c, slot, r0):
        g_buf = g_bufs[slot]
        h = h_ref[0, r0:r0 + mb, :]
        ag = _dot(h, wag_ref[c])
        a_bufs[slot][r0:r0 + mb, :] = ag[:, :fc]
        g_buf[main + r0:main + r0 + mb, :] = ag[:, fc:]
        if two_d and r0 == 0:
            g_buf[SUBLANES:main, :] = jnp.where(has_prev, _dot(hp_ref[0], wag_ref[c, :, fc:]), 0.0)
        if two_d and r0 + mb == tm:
            g_buf[main + tm:SUBLANES + ext, :] = jnp.where(has_next, _dot(hn_ref[0], wag_ref[c, :, fc:]), 0.0)

    def finish_rows(c, slot, r0):
        g_buf = g_bufs[slot]
        acts = []
        for q0 in range(r0, r0 + mb, rb):
            pos = q0 + lax.broadcasted_iota(jnp.int32, (rb, 1), 0)
            if two_d:
                has_left = pos % GRID_W != 0
                has_right = pos % GRID_W != GRID_W - 1
            else:
                has_left = pos != 0
                has_right = pos != tm - 1

            def taps(dc, lo, hi):
                part = None
                for ri, off in enumerate(row_offsets):
                    tap = ri * 3 + dc + 1
                    start = SUBLANES + off + q0
                    term = _tile_op(jnp.multiply, g_buf[start + lo:start + hi, :], cw_ref[c, tap])
                    part = term if part is None else part + term
                return part

            left = pltpu.roll(taps(-1, -SUBLANES, rb), 1, axis=0)[SUBLANES:]
            right = pltpu.roll(taps(1, 0, rb + SUBLANES), rb + SUBLANES - 1, axis=0)[:rb]
            conv = taps(0, 0, rb) + jnp.where(has_left, left, 0.0) + jnp.where(has_right, right, 0.0)
            act = a_bufs[slot][q0:q0 + rb, :] * _gelu(_tile_op(jnp.add, conv, cb_ref[c]))
            acts.append(act.astype(BF16))
        acc_ref[r0:r0 + mb, :] += _dot(jnp.concatenate(acts, axis=0), wd_ref[c])

    nb = tm // mb
    lead = min(2, nb)
    total = nchunks * nb

    def run_item(chunk0, rel, with_up):
        if with_up:
            co, k = divmod(rel + lead, nb)
            up_rows(chunk0 + co, co % 2, k * mb)
        co, k = divmod(rel, nb)
        finish_rows(chunk0 + co, co % 2, k * mb)

    for n in range(lead):
        up_rows(n // nb, (n // nb) % 2, (n % nb) * mb)
    npairs = (nchunks - 1) // 2

    def pair(j, carry):
        for rel in range(2 * nb):
            run_item(2 * j, rel, True)
        return carry

    lax.fori_loop(0, npairs, pair, 0)
    for n in range(npairs * 2 * nb, total):
        run_item(2 * npairs, n - npairs * 2 * nb, n + lead < total)
    y = x_ref[0] + g2_ref[0] * acc_ref[...]
    if final:
        ms = jnp.mean(y * y, axis=-1, keepdims=True)
        y = y * lax.rsqrt(ms + EPS) * fg_ref[...]
    o_ref[0] = y


def _ffn_call(h2, x, gate2, w_ag, conv_taps, conv_bias, w_d, two_d, final_gain=None):
    b, t, d = x.shape
    nchunks, fc, _ = w_d.shape
    tok = lambda i, j: (i, j, 0)
    if two_d:
        tm = min(FFN_ROWS, t)
        rpt = tm // GRID_W
        nrows = t // GRID_W
        h_specs = [
            pl.BlockSpec((1, tm, d), tok),
            pl.BlockSpec((1, GRID_W, d), lambda i, j: (i, jnp.maximum(j * rpt - 1, 0), 0)),
            pl.BlockSpec((1, GRID_W, d), lambda i, j: (i, jnp.minimum((j + 1) * rpt, nrows - 1), 0)),
        ]
        h_args = [h2, h2, h2]
        ext = tm + 2 * GRID_W
    else:
        tm = t
        h_specs = [pl.BlockSpec((1, tm, d), tok)]
        h_args = [h2]
        ext = tm
    ntaps = conv_taps.shape[1]
    in_specs = h_specs + [
        pl.BlockSpec((1, tm, d), tok),
        pl.BlockSpec((1, 1, d), _mod_map(gate2, b)),
        _const_spec((nchunks, d, 2 * fc)),
        _const_spec((nchunks, ntaps, SUBLANES, fc)), _const_spec((nchunks, SUBLANES, fc)),
        _const_spec((nchunks, fc, d)),
    ]
    args = h_args + [x, gate2, w_ag, conv_taps, conv_bias, w_d]
    if final_gain is not None:
        in_specs.append(_const_spec((1, d)))
        args.append(final_gain)
    return pl.pallas_call(
        functools.partial(_ffn_kernel, two_d=two_d, final=final_gain is not None),
        grid=(b, t // tm),
        in_specs=in_specs,
        out_specs=pl.BlockSpec((1, tm, d), tok),
        out_shape=jax.ShapeDtypeStruct((b, t, d), F32),
        scratch_shapes=[pltpu.VMEM((tm, fc), F32), pltpu.VMEM((tm, fc), F32),
                        pltpu.VMEM((ext + 2 * SUBLANES, fc), F32), pltpu.VMEM((ext + 2 * SUBLANES, fc), F32),
                        pltpu.VMEM((tm, d), F32)],
        compiler_params=_params("parallel", "parallel"),
        name="ffn" if two_d else "ffn_ctx",
    )(*args)


def _sublane_tiles(a):
    return jnp.broadcast_to(a[..., None, :], a.shape[:-1] + (SUBLANES, a.shape[-1]))


def _chunked_cols(w, fc):
    d, f = w.shape
    return w.reshape(d, f // fc, fc).transpose(1, 0, 2)


def kernel(x, c, ctx, c_ctx, w_ada, b_ada, norm_mix, norm_ffn, w_in, lb_logits_fwd, lb_logits_bwd, hg_norm,
           sgu_norm_g, sgu_norm_b, w_spatial, b_spatial, w_out, w_up, conv_w, conv_b, w_down, norm_final):
    bsz, t, d = x.shape
    depth = w_ada.shape[0]
    f = w_down.shape[1]
    w = HG_WIDTH
    assert t % 512 == 0 and ctx.shape[1] % 256 == 0 and f % FFN_CHUNK == 0

    cond_rows = -(-(bsz + 1) // 8) * 8
    cond = jnp.concatenate([c, c_ctx[None, :], jnp.zeros((cond_rows - bsz - 1, d), F32)], axis=0)
    ada = _ada_call(cond, w_ada, b_ada)

    zeros_state = jnp.zeros((bsz, HG_HEADS, HG_DIM, HG_DIM), F32)
    lbs = (lb_logits_fwd, lb_logits_bwd)
    xc = ctx
    for l in range(depth):
        last = l == depth - 1
        mods = ada[l].reshape(cond_rows, 6, d)
        sh1, sc1, g1, sh2, sc2, g2 = (mods[:bsz, k][:, None, :] for k in range(6))
        csh1, csc1, cg1, csh2, csc2, cg2 = (mods[bsz:bsz + 1, k][:, None, :] for k in range(6))
        w_in_l = w_in[l].astype(BF16)
        w_guv = w_in_l[:, 4 * w:]
        gain_mix = norm_mix[l][None, :]
        gain_ffn = norm_ffn[l][None, :]
        w_out_l = w_out[l].astype(BF16)
        w_up_l = w_up[l].astype(BF16)
        w_ag = jnp.concatenate([_chunked_cols(w_up_l[:, :f], FFN_CHUNK), _chunked_cols(w_up_l[:, f:], FFN_CHUNK)],
                               axis=2)
        w_d = w_down[l].astype(BF16).reshape(f // FFN_CHUNK, FFN_CHUNK, d)
        taps = _sublane_tiles(_chunked_cols(conv_w[l].reshape(9, f), FFN_CHUNK))
        cbias = _sublane_tiles(conv_b[l].reshape(f // FFN_CHUNK, FFN_CHUNK))
        mix_w = (w_guv, hg_norm[l][None, :], sgu_norm_g[l][None, :], sgu_norm_b[l][None, :],
                 w_spatial[l].astype(BF16), jnp.repeat(b_spatial[l].T, MLP_DIM, axis=1),
                 w_out_l)

        if last:
            pc = _in_proj_call(xc, csh1, csc1, gain_mix, w_in_l, 3 * w)
            hf_c, hb_c = _scan_call(pc, *lbs, zeros_state, zeros_state, l, False)
        else:
            pc = _in_proj_call(xc, csh1, csc1, gain_mix, w_in_l, 4 * w)
            oc, hf_c, hb_c = _scan_call(pc, *lbs, zeros_state, zeros_state, l, True)

        p = _in_proj_call(x, sh1, sc1, gain_mix, w_in_l, 4 * w)
        o, _, _ = _scan_call(p, *lbs, hf_c, hb_c, l, True)
        x_mid, h2 = _mix_out_call(o, x, sh1, sc1, gain_mix, *mix_w, g1, sh2, sc2, gain_ffn)
        x = _ffn_call(h2, x_mid, g2, w_ag, taps, cbias, w_d, True,
                      final_gain=norm_final[None, :] if last else None)

        if not last:
            xc_mid, hc2 = _mix_out_call(oc, xc, csh1, csc1, gain_mix, *mix_w, cg1, csh2, csc2, gain_ffn)
            xc = _ffn_call(hc2, xc_mid, cg2, w_ag, taps[:, 3:6], cbias, w_d, False)
    return x
```

```python
import functools

import jax
import jax.numpy as jnp
from jax import lax
from jax.experimental import pallas as pl
from jax.experimental.pallas import tpu as pltpu

F32 = jnp.float32
BF16 = jnp.bfloat16

HG_DIM = 128
HG_HEADS = 4
HG_WIDTH = HG_DIM * HG_HEADS
MLP_HEADS = 4
MLP_DIM = 128
MLP_WIDTH = MLP_HEADS * MLP_DIM
GRID_W = 64
CHUNK = 128
EPS = 1e-6
Q_SCALE = HG_DIM ** -0.5
SQRT_HALF = 0.7071067811865476
LOG2_E = 1.4426950408889634

IN_PROJ_ROWS = 512
IN_PROJ_COLS = 512
FFN_CHUNK = 256
FFN_ROWS = 1024
FFN_ROW_BLOCK = 128
FFN_MATMUL_ROWS = 256
MIX_ROWS = 1024
MIX_SUB_ROWS = 256
SUBLANES = 8
VMEM_LIMIT = 56 * 1024 * 1024


def _const_spec(shape):
    zeros = (0,) * len(shape)
    return pl.BlockSpec(shape, lambda *_: zeros, pipeline_mode=pl.Buffered(1))


def _params(*semantics):
    return pltpu.CompilerParams(dimension_semantics=semantics, vmem_limit_bytes=VMEM_LIMIT)


def _mod_map(mod, batch):
    if mod.shape[0] == batch and batch > 1:
        return lambda i, j: (i, 0, 0)
    return lambda i, j: (0, 0, 0)


def _gelu(x):
    return 0.5 * x * (1.0 + lax.erf(x * SQRT_HALF))


def _silu(x):
    return x * jax.nn.sigmoid(x)


def _dot(a, b):
    return jnp.dot(a, b, preferred_element_type=F32)


def _dot_nt(a, b):
    return lax.dot_general(a, b, (((1,), (1,)), ((), ())), preferred_element_type=F32)


def _tile_op(op, x, tile):
    rows, n = x.shape
    return op(x.reshape(rows // SUBLANES, SUBLANES, n), tile[None]).reshape(rows, n)


def _normed(x, gain, shift, scale):
    ms = jnp.mean(x * x, axis=-1, keepdims=True)
    h = x * lax.rsqrt(ms + EPS) * gain
    return (h * (1.0 + scale) + shift).astype(BF16)


def _ada_kernel(c_ref, w_ref, b_ref, o_ref):
    a = _silu(c_ref[...])
    a_hi = a.astype(BF16)
    a_lo = (a - a_hi.astype(F32)).astype(BF16)
    w = w_ref[0]
    w_hi = w.astype(BF16)
    w_lo = (w - w_hi.astype(F32)).astype(BF16)
    o_ref[0] = _dot(a_hi, w_hi) + _dot(a_hi, w_lo) + _dot(a_lo, w_hi) + b_ref[0]


def _ada_call(cc, w_ada, b_ada):
    depth, d, e = w_ada.shape
    rows = cc.shape[0]
    tn = 1536
    return pl.pallas_call(
        _ada_kernel,
        grid=(depth, e // tn),
        in_specs=[
            pl.BlockSpec((rows, d), lambda l, j: (0, 0)),
            pl.BlockSpec((1, d, tn), lambda l, j: (l, 0, j)),
            pl.BlockSpec((1, 1, tn), lambda l, j: (l, 0, j)),
        ],
        out_specs=pl.BlockSpec((1, rows, tn), lambda l, j: (l, 0, j)),
        out_shape=jax.ShapeDtypeStruct((depth, rows, e), F32),
        compiler_params=_params("parallel", "parallel"),
        name="ada",
    )(cc, w_ada, b_ada.reshape(depth, 1, e))


def _in_proj_kernel(x_ref, sh_ref, sc_ref, gain_ref, w_ref, o_ref):
    h = _normed(x_ref[0], gain_ref[...], sh_ref[0], sc_ref[0])
    for c0 in range(0, o_ref.shape[2], IN_PROJ_COLS):
        cols = slice(c0, c0 + IN_PROJ_COLS)
        o_ref[0, :, cols] = _dot(h, w_ref[:, cols])


def _in_proj_call(x, shift, scale, gain, w, ncols):
    b, t, d = x.shape
    tm = min(IN_PROJ_ROWS, t)
    return pl.pallas_call(
        _in_proj_kernel,
        grid=(b, t // tm),
        in_specs=[
            pl.BlockSpec((1, tm, d), lambda i, j: (i, j, 0)),
            pl.BlockSpec((1, 1, d), _mod_map(shift, b)),
            pl.BlockSpec((1, 1, d), _mod_map(scale, b)),
            _const_spec((1, d)),
            pl.BlockSpec((d, ncols), lambda i, j: (0, 0), pipeline_mode=pl.Buffered(1)),
        ],
        out_specs=pl.BlockSpec((1, tm, ncols), lambda i, j: (i, j, 0)),
        out_shape=jax.ShapeDtypeStruct((b, t, ncols), F32),
        compiler_params=_params("parallel", "parallel"),
        name="in_proj",
    )(x, shift, scale, gain, w)


def _low_level_exponent(b2, logf2, m, reverse):
    n, w = b2.shape
    pos = lax.broadcasted_iota(jnp.int32, (n, 1), 0)
    if m == 1:
        sel = (pos % 2 == 0) if reverse else (pos % 2 == 1)
        return jnp.where(sel, logf2, 0.0)
    if m == 2:
        r = pos % 4
        nxt = pltpu.roll(logf2, n - 1, axis=0)
        prv = pltpu.roll(logf2, 1, axis=0)
        if reverse:
            return jnp.where(r == 0, logf2 + nxt, jnp.where(r == 1, logf2, jnp.where(r == 2, 0.0, prv)))
        return jnp.where(r == 0, nxt, jnp.where(r == 1, 0.0, jnp.where(r == 2, logf2, logf2 + prv)))
    assert m == 4
    b3 = b2.reshape(n // SUBLANES, SUBLANES, w)
    mid = m if reverse else m - 1
    return (-jnp.abs(b3 - b3[:, mid:mid + 1, :])).reshape(n, w)


def _scan_direction(f_logit, v, q_raw, lb, tri, lvl, st_ref, o_ref, row0, b_scr, *, first, reverse):
    n, w = f_logit.shape
    if first:
        e = jnp.exp(-jnp.abs(f_logit))
        logf = jnp.minimum(f_logit, 0.0) - jnp.log1p(e)
        kk = jnp.where(f_logit >= 0.0, e, 1.0) / (1.0 + e)
    else:
        gate = lb + (1.0 - lb) * jax.nn.sigmoid(f_logit)
        logf = jnp.log(gate)
        kk = 1.0 - gate
    logf2 = logf * LOG2_E
    hi = logf2.astype(BF16)
    lo = (logf2 - hi.astype(F32)).astype(BF16)
    b2 = _dot(tri, hi) + _dot(tri, lo)
    last2 = b2[0:1, :] if reverse else b2[n - 1:n, :]
    k16 = kk.astype(BF16)
    k_end = k16 * jnp.exp2(last2 - b2).astype(BF16)
    state_decay = jnp.exp2(last2)
    v16 = v.astype(BF16)

    if o_ref is not None:
        q = _silu(q_raw) * Q_SCALE
        q16 = q.astype(BF16)
        q_in = q16 * jnp.exp2(b2).astype(BF16)
        heads = [slice(h * HG_DIM, (h + 1) * HG_DIM) for h in range(HG_HEADS)]
        diag_code = n.bit_length() - 1
        on_diag = lvl == diag_code
        scores = [jnp.where(on_diag, _dot_nt(q16[:, hs], k16[:, hs]), 0.0) for hs in heads]

        m, code = 1, 0
        while 2 * m <= SUBLANES:
            ex = jnp.exp2(_low_level_exponent(b2, logf2, m, reverse)).astype(BF16)
            q_m = q16 * ex
            k_m = k16 * ex
            at_level = lvl == code
            for h, hs in enumerate(heads):
                scores[h] = jnp.where(at_level, _dot_nt(q_m[:, hs], k_m[:, hs]), scores[h])
            m, code = 2 * m, code + 1

        b_scr[...] = b2
        while m < n:
            bases = range(0, n, 2 * m)
            q_parts, k_parts = [], []
            for base in bases:
                first_half, second_half = slice(base, base + m), slice(base + m, base + 2 * m)
                t_rows, s_rows = (first_half, second_half) if reverse else (second_half, first_half)
                r = base + (m if reverse else m - 1)
                bref = b_scr[r:r + 1, :]
                q_parts.append(q[t_rows] * jnp.exp2(b2[t_rows] - bref))
                k_s = kk[s_rows] * jnp.exp2(bref - b2[s_rows])
                k_parts += [jnp.zeros_like(k_s), k_s] if reverse else [k_s, jnp.zeros_like(k_s)]
            at_level = lvl == code
            q_t = jnp.concatenate(q_parts, axis=0).astype(BF16)
            k_s = jnp.concatenate(k_parts, axis=0).astype(BF16)
            for h, hs in enumerate(heads):
                s_t = _dot_nt(q_t[:, hs], k_s[:, hs])
                rows = []
                for j, base in enumerate(bases):
                    first_half, second_half = slice(base, base + m), slice(base + m, base + 2 * m)
                    t_rows, s_rows = (first_half, second_half) if reverse else (second_half, first_half)
                    new = jnp.where(at_level[t_rows], s_t[j * m:(j + 1) * m], scores[h][t_rows])
                    rows += [new, scores[h][s_rows]] if reverse else [scores[h][s_rows], new]
                scores[h] = jnp.concatenate(rows, axis=0)
            m, code = 2 * m, code + 1

    for h in range(HG_HEADS):
        hs = slice(h * HG_DIM, (h + 1) * HG_DIM)
        st = st_ref[0, h]
        if o_ref is not None:
            o_ref[0, pl.ds(row0, n), hs] += (_dot(scores[h].astype(BF16), v16[:, hs])
                                             + _dot_nt(q_in[:, hs], st.astype(BF16)))
        v_t = v[:, hs].T.astype(BF16)
        st_ref[0, h] = st * state_decay[:, hs] + _dot(v_t, k_end[:, hs])


def _lower_bound(logits_ref, layer):
    z = logits_ref[...]
    ez = jnp.exp(z - jnp.max(z, axis=0, keepdims=True))
    sm = ez / jnp.sum(ez, axis=0, keepdims=True)
    lb = jnp.zeros((1, z.shape[1]), F32)
    for j in range(1, layer + 1):
        lb = lb + sm[j:j + 1, :]
    return lb


def _scan_kernel(*refs, layer, want_o):
    if want_o:
        (ff_ref, fb_ref, vf_ref, vb_ref, qf_ref, qb_ref, lgf_ref, lgb_ref, trif_ref, trib_ref, lvlf_ref, lvlb_ref,
         h0f_ref, h0b_ref, o_ref, hf_ref, hb_ref, b_scr) = refs
    else:
        (ff_ref, fb_ref, vf_ref, vb_ref, lgf_ref, lgb_ref, trif_ref, trib_ref,
         h0f_ref, h0b_ref, hf_ref, hb_ref) = refs
        qf_ref = qb_ref = o_ref = lvlf_ref = lvlb_ref = b_scr = None
    step = pl.program_id(1)

    @pl.when(step == 0)
    def _():
        hf_ref[...] = h0f_ref[...]
        hb_ref[...] = h0b_ref[...]
        if want_o:
            o_ref[...] = jnp.zeros_like(o_ref)

    first = layer == 0
    fwd_row0 = pl.multiple_of(step * CHUNK, CHUNK)
    bwd_row0 = pl.multiple_of((pl.num_programs(1) - 1 - step) * CHUNK, CHUNK)
    for reverse, f_ref, v_ref, q_ref, lg_ref, tri_ref, lvl_ref, st_ref, row0 in (
            (False, ff_ref, vf_ref, qf_ref, lgf_ref, trif_ref, lvlf_ref, hf_ref, fwd_row0),
            (True, fb_ref, vb_ref, qb_ref, lgb_ref, trib_ref, lvlb_ref, hb_ref, bwd_row0)):
        lb = None if first else _lower_bound(lg_ref, layer)
        _scan_direction(
            f_ref[0], v_ref[0], q_ref[0] if want_o else None, lb, tri_ref[...],
            lvl_ref[...] if want_o else None, st_ref, o_ref, row0, b_scr, first=first, reverse=reverse)


def _scan_masks(n):
    t = jnp.arange(n, dtype=jnp.int32)[:, None]
    s = jnp.arange(n, dtype=jnp.int32)[None, :]
    x = t ^ s
    level = jnp.zeros((n, n), jnp.int32)
    for k in range(1, n.bit_length() - 1):
        level = jnp.where((x >> k) > 0, k, level)
    diag_code = n.bit_length() - 1
    lvl_f = jnp.where(t == s, diag_code, jnp.where(t > s, level, -1))
    return (t >= s).astype(BF16), (t <= s).astype(BF16), lvl_f, lvl_f.T


def _scan_call(p, lb_logits_f, lb_logits_b, h0f, h0b, layer, want_o):
    b, t, _ = p.shape
    nc = t // CHUNK
    w = HG_WIDTH
    tri_f, tri_b, lvl_f, lvl_b = _scan_masks(CHUNK)
    depth = lb_logits_f.shape[0]

    def col(block, rev):
        if rev:
            return pl.BlockSpec((1, CHUNK, w), lambda i, j: (i, nc - 1 - j, block))
        return pl.BlockSpec((1, CHUNK, w), lambda i, j: (i, j, block))

    state_spec = pl.BlockSpec((1, HG_HEADS, HG_DIM, HG_DIM), lambda i, j: (i, 0, 0, 0))
    state_shape = jax.ShapeDtypeStruct((b, HG_HEADS, HG_DIM, HG_DIM), F32)
    in_specs = [col(0, False), col(1, True), col(2, False), col(2, True)]
    args = [p, p, p, p]
    if want_o:
        in_specs += [col(3, False), col(3, True)]
        args += [p, p]
    in_specs += [_const_spec((depth, w)), _const_spec((depth, w)),
                 _const_spec((CHUNK, CHUNK)), _const_spec((CHUNK, CHUNK))]
    args += [lb_logits_f, lb_logits_b, tri_f, tri_b]
    if want_o:
        in_specs += [_const_spec((CHUNK, CHUNK)), _const_spec((CHUNK, CHUNK))]
        args += [lvl_f, lvl_b]
    in_specs += [state_spec, state_spec]
    args += [h0f, h0b]
    out_specs = [state_spec, state_spec]
    out_shape = [state_shape, state_shape]
    scratch = []
    if want_o:
        out_specs = [pl.BlockSpec((1, t, w), lambda i, j: (i, 0, 0))] + out_specs
        out_shape = [jax.ShapeDtypeStruct((b, t, w), F32)] + out_shape
        scratch = [pltpu.VMEM((CHUNK, w), F32)]
    return pl.pallas_call(
        functools.partial(_scan_kernel, layer=layer, want_o=want_o),
        grid=(b, nc),
        in_specs=in_specs,
        out_specs=out_specs,
        out_shape=out_shape,
        scratch_shapes=scratch,
        compiler_params=_params("parallel", "arbitrary"),
        name="scan" if want_o else "scan_state",
    )(*args)


def _mix_out_kernel(o_ref, x_ref, sh1_ref, sc1_ref, ng1_ref, wguv_ref, hgg_ref, lng_ref, lnb_ref, ws_ref, bs_ref,
                    wo_ref, g1_ref, sh2_ref, sc2_ref, ng2_ref, xo_ref, h2_ref):
    tm = x_ref.shape[1]
    w = HG_WIDTH
    sub = min(MIX_SUB_ROWS, tm)

    def project(r0):
        h = _normed(x_ref[0, r0:r0 + sub, :], ng1_ref[...], sh1_ref[0], sc1_ref[0])
        return _dot(h, wguv_ref[...])

    guv_next = project(0)
    for r0 in range(0, tm, sub):
        blk = slice(r0, r0 + sub)
        guv = guv_next
        if r0 + sub < tm:
            guv_next = project(r0 + sub)
        o = o_ref[0, blk, :]
        parts = []
        for h in range(HG_HEADS):
            oh = o[:, h * HG_DIM:(h + 1) * HG_DIM]
            parts.append(oh * lax.rsqrt(jnp.mean(oh * oh, axis=-1, keepdims=True) + EPS))
        o_hg = jnp.concatenate(parts, axis=-1) * hgg_ref[...] * _silu(guv[:, :w])

        u = _gelu(guv[:, w:w + MLP_WIDTH])
        v = _gelu(guv[:, w + MLP_WIDTH:])
        parts = []
        for h in range(MLP_HEADS):
            vh = v[:, h * MLP_DIM:(h + 1) * MLP_DIM]
            dv = vh - jnp.mean(vh, axis=-1, keepdims=True)
            parts.append(dv * lax.rsqrt(jnp.mean(dv * dv, axis=-1, keepdims=True) + EPS))
        vn = (jnp.concatenate(parts, axis=-1) * lng_ref[...] + lnb_ref[...]).astype(BF16)
        rows = []
        for c in range(sub // CHUNK):
            zc = [_dot(ws_ref[h], vn[c * CHUNK:(c + 1) * CHUNK, h * MLP_DIM:(h + 1) * MLP_DIM])
                  for h in range(MLP_HEADS)]
            rows.append(jnp.concatenate(zc, axis=-1) + bs_ref[...])
        o_mlp = u * jnp.concatenate(rows, axis=0)

        y = _dot(o_hg.astype(BF16), wo_ref[0:HG_WIDTH, :]) + _dot(o_mlp.astype(BF16), wo_ref[HG_WIDTH:, :])
        xn = x_ref[0, blk, :] + g1_ref[0] * y
        xo_ref[0, blk, :] = xn
        h2_ref[0, blk, :] = _normed(xn, ng2_ref[...], sh2_ref[0], sc2_ref[0])


def _mix_out_call(o, x, shift1, scale1, gain1, w_guv, hg_gain, ln_g, ln_b, w_s, bias_full, w_out,
                  gate1, shift2, scale2, gain2):
    b, t, d = x.shape
    tm = min(MIX_ROWS, t)
    w = HG_WIDTH
    tok = lambda i, j: (i, j, 0)
    mod = lambda m: pl.BlockSpec((1, 1, d), _mod_map(m, b))
    return pl.pallas_call(
        _mix_out_kernel,
        grid=(b, t // tm),
        in_specs=[
            pl.BlockSpec((1, tm, w), tok),
            pl.BlockSpec((1, tm, d), tok),
            mod(shift1), mod(scale1), _const_spec((1, d)),
            _const_spec(w_guv.shape),
            _const_spec((1, w)), _const_spec((1, w)), _const_spec((1, w)),
            _const_spec((MLP_HEADS, CHUNK, CHUNK)),
            _const_spec((CHUNK, w)),
            _const_spec((d, d)),
            mod(gate1), mod(shift2), mod(scale2),
            _const_spec((1, d)),
        ],
        out_specs=[pl.BlockSpec((1, tm, d), tok), pl.BlockSpec((1, tm, d), tok)],
        out_shape=[jax.ShapeDtypeStruct((b, t, d), F32), jax.ShapeDtypeStruct((b, t, d), BF16)],
        compiler_params=_params("parallel", "parallel"),
        name="mix_out",
    )(o, x, shift1, scale1, gain1, w_guv, hg_gain, ln_g, ln_b, w_s, bias_full, w_out, gate1, shift2, scale2, gain2)


def _ffn_kernel(*refs, two_d, final):
    if two_d:
        h_ref, hp_ref, hn_ref = refs[:3]
        refs = refs[3:]
    else:
        h_ref = refs[0]
        refs = refs[1:]
    x_ref, g2_ref, wag_ref, cw_ref, cb_ref, wd_ref = refs[:6]
    refs = refs[6:]
    if final:
        fg_ref = refs[0]
        refs = refs[1:]
    o_ref, a0, a1, g0, g1, acc_ref = refs
    a_bufs, g_bufs = (a0, a1), (g0, g1)

    tm = x_ref.shape[1]
    nchunks, fc, _ = wd_ref.shape
    mb = min(FFN_MATMUL_ROWS, tm)
    rb = min(FFN_ROW_BLOCK, tm)
    if two_d:
        has_prev = pl.program_id(1) > 0
        has_next = pl.program_id(1) < pl.num_programs(1) - 1
        row_offsets = (0, GRID_W, 2 * GRID_W)
        ext = tm + 2 * GRID_W
        main = SUBLANES + GRID_W
    else:
        row_offsets = (0,)
        ext = tm
        main = SUBLANES
    acc_ref[...] = jnp.zeros_like(acc_ref)
    for g_buf in g_bufs:
        g_buf[0:SUBLANES, :] = jnp.zeros((SUBLANES, fc), F32)
        g_buf[SUBLANES + ext:, :] = jnp.zeros((SUBLANES, fc), F32)

    def up_rows(c, slot, r0):
        g_buf = g_bufs[slot]
        h = h_ref[0, r0:r0 + mb, :]
        ag = _dot(h, wag_ref[c])
        a_bufs[slot][r0:r0 + mb, :] = ag[:, :fc]
        g_buf[main + r0:main + r0 + mb, :] = ag[:, fc:]
        if two_d and r0 == 0:
            g_buf[SUBLANES:main, :] = jnp.where(has_prev, _dot(hp_ref[0], wag_ref[c, :, fc:]), 0.0)
        if two_d and r0 + mb == tm:
            g_buf[main + tm:SUBLANES + ext, :] = jnp.where(has_next, _dot(hn_ref[0], wag_ref[c, :, fc:]), 0.0)

    def finish_rows(c, slot, r0):
        g_buf = g_bufs[slot]
        acts = []
        for q0 in range(r0, r0 + mb, rb):
            pos = q0 + lax.broadcasted_iota(jnp.int32, (rb, 1), 0)
            if two_d:
                has_left = pos % GRID_W != 0
                has_right = pos % GRID_W != GRID_W - 1
            else:
                has_left = pos != 0
                has_right = pos != tm - 1

            def taps(dc, lo, hi):
                part = None
                for ri, off in enumerate(row_offsets):
                    tap = ri * 3 + dc + 1
                    start = SUBLANES + off + q0
                    term = _tile_op(jnp.multiply, g_buf[start + lo:start + hi, :], cw_ref[c, tap])
                    part = term if part is None else part + term
                return part

            left = pltpu.roll(taps(-1, -SUBLANES, rb), 1, axis=0)[SUBLANES:]
            right = pltpu.roll(taps(1, 0, rb + SUBLANES), rb + SUBLANES - 1, axis=0)[:rb]
            conv = taps(0, 0, rb) + jnp.where(has_left, left, 0.0) + jnp.where(has_right, right, 0.0)
            act = a_bufs[slot][q0:q0 + rb, :] * _gelu(_tile_op(jnp.add, conv, cb_ref[c]))
            acts.append(act.astype(BF16))
        acc_ref[r0:r0 + mb, :] += _dot(jnp.concatenate(acts, axis=0), wd_ref[c])

    nb = tm // mb
    lead = min(2, nb)
    total = nchunks * nb

    def run_item(chunk0, rel, with_up):
        if with_up:
            co, k = divmod(rel + lead, nb)
            up_rows(chunk0 + co, co % 2, k * mb)
        co, k = divmod(rel, nb)
        finish_rows(chunk0 + co, co % 2, k * mb)

    for n in range(lead):
        up_rows(n // nb, (n // nb) % 2, (n % nb) * mb)
    npairs = (nchunks - 1) // 2

    def pair(j, carry):
        for rel in range(2 * nb):
            run_item(2 * j, rel, True)
        return carry

    lax.fori_loop(0, npairs, pair, 0)
    for n in range(npairs * 2 * nb, total):
        run_item(2 * npairs, n - npairs * 2 * nb, n + lead < total)
    y = x_ref[0] + g2_ref[0] * acc_ref[...]
    if final:
        ms = jnp.mean(y * y, axis=-1, keepdims=True)
        y = y * lax.rsqrt(ms + EPS) * fg_ref[...]
    o_ref[0] = y


def _ffn_call(h2, x, gate2, w_ag, conv_taps, conv_bias, w_d, two_d, final_gain=None):
    b, t, d = x.shape
    nchunks, fc, _ = w_d.shape
    tok = lambda i, j: (i, j, 0)
    if two_d:
        tm = min(FFN_ROWS, t)
        rpt = tm // GRID_W
        nrows = t // GRID_W
        h_specs = [
            pl.BlockSpec((1, tm, d), tok),
            pl.BlockSpec((1, GRID_W, d), lambda i, j: (i, jnp.maximum(j * rpt - 1, 0), 0)),
            pl.BlockSpec((1, GRID_W, d), lambda i, j: (i, jnp.minimum((j + 1) * rpt, nrows - 1), 0)),
        ]
        h_args = [h2, h2, h2]
        ext = tm + 2 * GRID_W
    else:
        tm = t
        h_specs = [pl.BlockSpec((1, tm, d), tok)]
        h_args = [h2]
        ext = tm
    ntaps = conv_taps.shape[1]
    in_specs = h_specs + [
        pl.BlockSpec((1, tm, d), tok),
        pl.BlockSpec((1, 1, d), _mod_map(gate2, b)),
        _const_spec((nchunks, d, 2 * fc)),
        _const_spec((nchunks, ntaps, SUBLANES, fc)), _const_spec((nchunks, SUBLANES, fc)),
        _const_spec((nchunks, fc, d)),
    ]
    args = h_args + [x, gate2, w_ag, conv_taps, conv_bias, w_d]
    if final_gain is not None:
        in_specs.append(_const_spec((1, d)))
        args.append(final_gain)
    return pl.pallas_call(
        functools.partial(_ffn_kernel, two_d=two_d, final=final_gain is not None),
        grid=(b, t // tm),
        in_specs=in_specs,
        out_specs=pl.BlockSpec((1, tm, d), tok),
        out_shape=jax.ShapeDtypeStruct((b, t, d), F32),
        scratch_shapes=[pltpu.VMEM((tm, fc), F32), pltpu.VMEM((tm, fc), F32),
                        pltpu.VMEM((ext + 2 * SUBLANES, fc), F32), pltpu.VMEM((ext + 2 * SUBLANES, fc), F32),
                        pltpu.VMEM((tm, d), F32)],
        compiler_params=_params("parallel", "parallel"),
        name="ffn" if two_d else "ffn_ctx",
    )(*args)


def _sublane_tiles(a):
    return jnp.broadcast_to(a[..., None, :], a.shape[:-1] + (SUBLANES, a.shape[-1]))


def _chunked_cols(w, fc):
    d, f = w.shape
    return w.reshape(d, f // fc, fc).transpose(1, 0, 2)


def kernel(x, c, ctx, c_ctx, w_ada, b_ada, norm_mix, norm_ffn, w_in, lb_logits_fwd, lb_logits_bwd, hg_norm,
           sgu_norm_g, sgu_norm_b, w_spatial, b_spatial, w_out, w_up, conv_w, conv_b, w_down, norm_final):
    bsz, t, d = x.shape
    depth = w_ada.shape[0]
    f = w_down.shape[1]
    w = HG_WIDTH
    assert t % 512 == 0 and ctx.shape[1] % 256 == 0 and f % FFN_CHUNK == 0

    cond_rows = -(-(bsz + 1) // 8) * 8
    cond = jnp.concatenate([c, c_ctx[None, :], jnp.zeros((cond_rows - bsz - 1, d), F32)], axis=0)
    ada = _ada_call(cond, w_ada, b_ada)

    zeros_state = jnp.zeros((bsz, HG_HEADS, HG_DIM, HG_DIM), F32)
    lbs = (lb_logits_fwd, lb_logits_bwd)
    xc = ctx
    for l in range(depth):
        last = l == depth - 1
        mods = ada[l].reshape(cond_rows, 6, d)
        sh1, sc1, g1, sh2, sc2, g2 = (mods[:bsz, k][:, None, :] for k in range(6))
        csh1, csc1, cg1, csh2, csc2, cg2 = (mods[bsz:bsz + 1, k][:, None, :] for k in range(6))
        w_in_l = w_in[l].astype(BF16)
        w_guv = w_in_l[:, 4 * w:]
        gain_mix = norm_mix[l][None, :]
        gain_ffn = norm_ffn[l][None, :]
        w_out_l = w_out[l].astype(BF16)
        nch = f // FFN_CHUNK
        w_ag = (w_up[l].astype(BF16).reshape(d, 2, nch, FFN_CHUNK).transpose(2, 0, 1, 3)
                .reshape(nch, d, 2 * FFN_CHUNK))
        w_d = w_down[l].astype(BF16).reshape(f // FFN_CHUNK, FFN_CHUNK, d)
        taps = _sublane_tiles(_chunked_cols(conv_w[l].reshape(9, f), FFN_CHUNK))
        cbias = _sublane_tiles(conv_b[l].reshape(f // FFN_CHUNK, FFN_CHUNK))
        mix_w = (w_guv, hg_norm[l][None, :], sgu_norm_g[l][None, :], sgu_norm_b[l][None, :],
                 w_spatial[l].astype(BF16), jnp.repeat(b_spatial[l].T, MLP_DIM, axis=1),
                 w_out_l)

        if last:
            pc = _in_proj_call(xc, csh1, csc1, gain_mix, w_in_l, 3 * w)
            hf_c, hb_c = _scan_call(pc, *lbs, zeros_state, zeros_state, l, False)
        else:
            pc = _in_proj_call(xc, csh1, csc1, gain_mix, w_in_l, 4 * w)
            oc, hf_c, hb_c = _scan_call(pc, *lbs, zeros_state, zeros_state, l, True)

        p = _in_proj_call(x, sh1, sc1, gain_mix, w_in_l, 4 * w)
        o, _, _ = _scan_call(p, *lbs, hf_c, hb_c, l, True)
        x_mid, h2 = _mix_out_call(o, x, sh1, sc1, gain_mix, *mix_w, g1, sh2, sc2, gain_ffn)
        x = _ffn_call(h2, x_mid, g2, w_ag, taps, cbias, w_d, True,
                      final_gain=norm_final[None, :] if last else None)

        if not last:
            xc_mid, hc2 = _mix_out_call(oc, xc, csh1, csc1, gain_mix, *mix_w, cg1, csh2, csc2, gain_ffn)
            xc = _ffn_call(hc2, xc_mid, cg2, w_ag, taps[:, 3:6], cbias, w_d, False)
    return x
```

```python
import functools

import jax
import jax.numpy as jnp
from jax import lax
from jax.experimental import pallas as pl
from jax.experimental.pallas import tpu as pltpu

F32 = jnp.float32
BF16 = jnp.bfloat16

HG_DIM = 128
HG_HEADS = 4
HG_WIDTH = HG_DIM * HG_HEADS
MLP_HEADS = 4
MLP_DIM = 128
MLP_WIDTH = MLP_HEADS * MLP_DIM
GRID_W = 64
CHUNK = 128
EPS = 1e-6
Q_SCALE = HG_DIM ** -0.5
SQRT_HALF = 0.7071067811865476
LOG2_E = 1.4426950408889634

IN_PROJ_ROWS = 512
IN_PROJ_COLS = 512
FFN_CHUNK = 256
FFN_ROWS = 1024
FFN_ROW_BLOCK = 128
FFN_MATMUL_ROWS = 256
MIX_ROWS = 1024
MIX_SUB_ROWS = 256
SUBLANES = 8
VMEM_LIMIT = 56 * 1024 * 1024


def _const_spec(shape):
    zeros = (0,) * len(shape)
    return pl.BlockSpec(shape, lambda *_: zeros, pipeline_mode=pl.Buffered(1))


def _params(*semantics):
    return pltpu.CompilerParams(dimension_semantics=semantics, vmem_limit_bytes=VMEM_LIMIT)


def _mod_map(mod, batch):
    if mod.shape[0] == batch and batch > 1:
        return lambda i, j: (i, 0, 0)
    return lambda i, j: (0, 0, 0)


def _gelu(x):
    return 0.5 * x * (1.0 + lax.erf(x * SQRT_HALF))


def _silu(x):
    return x * jax.nn.sigmoid(x)


def _dot(a, b):
    return jnp.dot(a, b, preferred_element_type=F32)


def _dot_nt(a, b):
    return lax.dot_general(a, b, (((1,), (1,)), ((), ())), preferred_element_type=F32)


def _tile_op(op, x, tile):
    rows, n = x.shape
    return op(x.reshape(rows // SUBLANES, SUBLANES, n), tile[None]).reshape(rows, n)


def _normed(x, gain, shift, scale):
    ms = jnp.mean(x * x, axis=-1, keepdims=True)
    h = x * lax.rsqrt(ms + EPS) * gain
    return (h * (1.0 + scale) + shift).astype(BF16)


def _ada_kernel(c_ref, w_ref, b_ref, o_ref):
    a = _silu(c_ref[...])
    a_hi = a.astype(BF16)
    a_lo = (a - a_hi.astype(F32)).astype(BF16)
    w = w_ref[0]
    w_hi = w.astype(BF16)
    w_lo = (w - w_hi.astype(F32)).astype(BF16)
    o_ref[0] = _dot(a_hi, w_hi) + _dot(a_hi, w_lo) + _dot(a_lo, w_hi) + b_ref[0]


def _ada_call(cc, w_ada, b_ada):
    depth, d, e = w_ada.shape
    rows = cc.shape[0]
    tn = 1536
    return pl.pallas_call(
        _ada_kernel,
        grid=(depth, e // tn),
        in_specs=[
            pl.BlockSpec((rows, d), lambda l, j: (0, 0)),
            pl.BlockSpec((1, d, tn), lambda l, j: (l, 0, j)),
            pl.BlockSpec((1, 1, tn), lambda l, j: (l, 0, j)),
        ],
        out_specs=pl.BlockSpec((1, rows, tn), lambda l, j: (l, 0, j)),
        out_shape=jax.ShapeDtypeStruct((depth, rows, e), F32),
        compiler_params=_params("parallel", "parallel"),
        name="ada",
    )(cc, w_ada, b_ada.reshape(depth, 1, e))


def _in_proj_kernel(x_ref, sh_ref, sc_ref, gain_ref, w_ref, o_ref):
    h = _normed(x_ref[0], gain_ref[...], sh_ref[0], sc_ref[0])
    for c0 in range(0, o_ref.shape[2], IN_PROJ_COLS):
        cols = slice(c0, c0 + IN_PROJ_COLS)
        o_ref[0, :, cols] = _dot(h, w_ref[:, cols])


def _in_proj_call(x, shift, scale, gain, w, ncols):
    b, t, d = x.shape
    tm = min(IN_PROJ_ROWS, t)
    return pl.pallas_call(
        _in_proj_kernel,
        grid=(b, t // tm),
        in_specs=[
            pl.BlockSpec((1, tm, d), lambda i, j: (i, j, 0)),
            pl.BlockSpec((1, 1, d), _mod_map(shift, b)),
            pl.BlockSpec((1, 1, d), _mod_map(scale, b)),
            _const_spec((1, d)),
            pl.BlockSpec((d, ncols), lambda i, j: (0, 0), pipeline_mode=pl.Buffered(1)),
        ],
        out_specs=pl.BlockSpec((1, tm, ncols), lambda i, j: (i, j, 0)),
        out_shape=jax.ShapeDtypeStruct((b, t, ncols), F32),
        compiler_params=_params("parallel", "parallel"),
        name="in_proj",
    )(x, shift, scale, gain, w)


def _low_level_exponent(b2, logf2, m, reverse):
    n, w = b2.shape
    pos = lax.broadcasted_iota(jnp.int32, (n, 1), 0)
    if m == 1:
        sel = (pos % 2 == 0) if reverse else (pos % 2 == 1)
        return jnp.where(sel, logf2, 0.0)
    if m == 2:
        r = pos % 4
        nxt = pltpu.roll(logf2, n - 1, axis=0)
        prv = pltpu.roll(logf2, 1, axis=0)
        if reverse:
            return jnp.where(r == 0, logf2 + nxt, jnp.where(r == 1, logf2, jnp.where(r == 2, 0.0, prv)))
        return jnp.where(r == 0, nxt, jnp.where(r == 1, 0.0, jnp.where(r == 2, logf2, logf2 + prv)))
    assert m == 4
    b3 = b2.reshape(n // SUBLANES, SUBLANES, w)
    mid = m if reverse else m - 1
    return (-jnp.abs(b3 - b3[:, mid:mid + 1, :])).reshape(n, w)


def _scan_direction(f_logit, v, q_raw, lb, tri, lvl, st_ref, o_ref, row0, b_scr, *, first, reverse):
    n, w = f_logit.shape
    if first:
        e = jnp.exp(-jnp.abs(f_logit))
        logf = jnp.minimum(f_logit, 0.0) - jnp.log1p(e)
        kk = jnp.where(f_logit >= 0.0, e, 1.0) / (1.0 + e)
    else:
        gate = lb + (1.0 - lb) * jax.nn.sigmoid(f_logit)
        logf = jnp.log(gate)
        kk = 1.0 - gate
    logf2 = logf * LOG2_E
    hi = logf2.astype(BF16)
    lo = (logf2 - hi.astype(F32)).astype(BF16)
    b2 = _dot(tri, hi) + _dot(tri, lo)
    last2 = b2[0:1, :] if reverse else b2[n - 1:n, :]
    k16 = kk.astype(BF16)
    k_end = k16 * jnp.exp2(last2 - b2).astype(BF16)
    state_decay = jnp.exp2(last2)
    v16 = v.astype(BF16)

    if o_ref is not None:
        q = _silu(q_raw) * Q_SCALE
        q16 = q.astype(BF16)
        q_in = q16 * jnp.exp2(b2).astype(BF16)
        heads = [slice(h * HG_DIM, (h + 1) * HG_DIM) for h in range(HG_HEADS)]
        diag_code = n.bit_length() - 1
        on_diag = lvl == diag_code
        scores = [jnp.where(on_diag, _dot_nt(q16[:, hs], k16[:, hs]), 0.0) for hs in heads]

        m, code = 1, 0
        while 2 * m <= SUBLANES:
            ex = jnp.exp2(_low_level_exponent(b2, logf2, m, reverse)).astype(BF16)
            q_m = q16 * ex
            k_m = k16 * ex
            at_level = lvl == code
            for h, hs in enumerate(heads):
                scores[h] = jnp.where(at_level, _dot_nt(q_m[:, hs], k_m[:, hs]), scores[h])
            m, code = 2 * m, code + 1

        b_scr[...] = b2
        while m < n:
            bases = range(0, n, 2 * m)
            q_parts, k_parts = [], []
            for base in bases:
                first_half, second_half = slice(base, base + m), slice(base + m, base + 2 * m)
                t_rows, s_rows = (first_half, second_half) if reverse else (second_half, first_half)
                r = base + (m if reverse else m - 1)
                bref = b_scr[r:r + 1, :]
                q_parts.append(q[t_rows] * jnp.exp2(b2[t_rows] - bref))
                k_s = kk[s_rows] * jnp.exp2(bref - b2[s_rows])
                k_parts += [jnp.zeros_like(k_s), k_s] if reverse else [k_s, jnp.zeros_like(k_s)]
            at_level = lvl == code
            q_t = jnp.concatenate(q_parts, axis=0).astype(BF16)
            k_s = jnp.concatenate(k_parts, axis=0).astype(BF16)
            for h, hs in enumerate(heads):
                s_t = _dot_nt(q_t[:, hs], k_s[:, hs])
                rows = []
                for j, base in enumerate(bases):
                    first_half, second_half = slice(base, base + m), slice(base + m, base + 2 * m)
                    t_rows, s_rows = (first_half, second_half) if reverse else (second_half, first_half)
                    new = jnp.where(at_level[t_rows], s_t[j * m:(j + 1) * m], scores[h][t_rows])
                    rows += [new, scores[h][s_rows]] if reverse else [scores[h][s_rows], new]
                scores[h] = jnp.concatenate(rows, axis=0)
            m, code = 2 * m, code + 1

    for h in range(HG_HEADS):
        hs = slice(h * HG_DIM, (h + 1) * HG_DIM)
        st = st_ref[0, h]
        if o_ref is not None:
            o_ref[0, pl.ds(row0, n), hs] += (_dot(scores[h].astype(BF16), v16[:, hs])
                                             + _dot_nt(q_in[:, hs], st.astype(BF16)))
        v_t = v[:, hs].T.astype(BF16)
        st_ref[0, h] = st * state_decay[:, hs] + _dot(v_t, k_end[:, hs])


def _lower_bound(logits_ref, layer):
    z = logits_ref[...]
    ez = jnp.exp(z - jnp.max(z, axis=0, keepdims=True))
    sm = ez / jnp.sum(ez, axis=0, keepdims=True)
    lb = jnp.zeros((1, z.shape[1]), F32)
    for j in range(1, layer + 1):
        lb = lb + sm[j:j + 1, :]
    return lb


def _scan_kernel(*refs, layer, want_o):
    if want_o:
        (ff_ref, fb_ref, vf_ref, vb_ref, qf_ref, qb_ref, lgf_ref, lgb_ref, trif_ref, trib_ref, lvlf_ref, lvlb_ref,
         h0f_ref, h0b_ref, o_ref, hf_ref, hb_ref, b_scr) = refs
    else:
        (ff_ref, fb_ref, vf_ref, vb_ref, lgf_ref, lgb_ref, trif_ref, trib_ref,
         h0f_ref, h0b_ref, hf_ref, hb_ref) = refs
        qf_ref = qb_ref = o_ref = lvlf_ref = lvlb_ref = b_scr = None
    step = pl.program_id(1)

    @pl.when(step == 0)
    def _():
        hf_ref[...] = h0f_ref[...]
        hb_ref[...] = h0b_ref[...]
        if want_o:
            o_ref[...] = jnp.zeros_like(o_ref)

    first = layer == 0
    fwd_row0 = pl.multiple_of(step * CHUNK, CHUNK)
    bwd_row0 = pl.multiple_of((pl.num_programs(1) - 1 - step) * CHUNK, CHUNK)
    for reverse, f_ref, v_ref, q_ref, lg_ref, tri_ref, lvl_ref, st_ref, row0 in (
            (False, ff_ref, vf_ref, qf_ref, lgf_ref, trif_ref, lvlf_ref, hf_ref, fwd_row0),
            (True, fb_ref, vb_ref, qb_ref, lgb_ref, trib_ref, lvlb_ref, hb_ref, bwd_row0)):
        lb = None if first else _lower_bound(lg_ref, layer)
        _scan_direction(
            f_ref[0], v_ref[0], q_ref[0] if want_o else None, lb, tri_ref[...],
            lvl_ref[...] if want_o else None, st_ref, o_ref, row0, b_scr, first=first, reverse=reverse)


def _scan_masks(n):
    t = jnp.arange(n, dtype=jnp.int32)[:, None]
    s = jnp.arange(n, dtype=jnp.int32)[None, :]
    x = t ^ s
    level = jnp.zeros((n, n), jnp.int32)
    for k in range(1, n.bit_length() - 1):
        level = jnp.where((x >> k) > 0, k, level)
    diag_code = n.bit_length() - 1
    lvl_f = jnp.where(t == s, diag_code, jnp.where(t > s, level, -1))
    return (t >= s).astype(BF16), (t <= s).astype(BF16), lvl_f, lvl_f.T


def _scan_call(p, lb_logits_f, lb_logits_b, h0f, h0b, layer, want_o):
    b, t, _ = p.shape
    nc = t // CHUNK
    w = HG_WIDTH
    tri_f, tri_b, lvl_f, lvl_b = _scan_masks(CHUNK)
    depth = lb_logits_f.shape[0]

    def col(block, rev):
        if rev:
            return pl.BlockSpec((1, CHUNK, w), lambda i, j: (i, nc - 1 - j, block))
        return pl.BlockSpec((1, CHUNK, w), lambda i, j: (i, j, block))

    state_spec = pl.BlockSpec((1, HG_HEADS, HG_DIM, HG_DIM), lambda i, j: (i, 0, 0, 0))
    state_shape = jax.ShapeDtypeStruct((b, HG_HEADS, HG_DIM, HG_DIM), F32)
    in_specs = [col(0, False), col(1, True), col(2, False), col(2, True)]
    args = [p, p, p, p]
    if want_o:
        in_specs += [col(3, False), col(3, True)]
        args += [p, p]
    in_specs += [_const_spec((depth, w)), _const_spec((depth, w)),
                 _const_spec((CHUNK, CHUNK)), _const_spec((CHUNK, CHUNK))]
    args += [lb_logits_f, lb_logits_b, tri_f, tri_b]
    if want_o:
        in_specs += [_const_spec((CHUNK, CHUNK)), _const_spec((CHUNK, CHUNK))]
        args += [lvl_f, lvl_b]
    in_specs += [state_spec, state_spec]
    args += [h0f, h0b]
    out_specs = [state_spec, state_spec]
    out_shape = [state_shape, state_shape]
    scratch = []
    if want_o:
        out_specs = [pl.BlockSpec((1, t, w), lambda i, j: (i, 0, 0))] + out_specs
        out_shape = [jax.ShapeDtypeStruct((b, t, w), F32)] + out_shape
        scratch = [pltpu.VMEM((CHUNK, w), F32)]
    return pl.pallas_call(
        functools.partial(_scan_kernel, layer=layer, want_o=want_o),
        grid=(b, nc),
        in_specs=in_specs,
        out_specs=out_specs,
        out_shape=out_shape,
        scratch_shapes=scratch,
        compiler_params=_params("parallel", "arbitrary"),
        name="scan" if want_o else "scan_state",
    )(*args)


def _mix_out_kernel(o_ref, x_ref, sh1_ref, sc1_ref, ng1_ref, wguv_ref, hgg_ref, lng_ref, lnb_ref, ws_ref, bs_ref,
                    wo_ref, g1_ref, sh2_ref, sc2_ref, ng2_ref, xo_ref, h2_ref):
    tm = x_ref.shape[1]
    w = HG_WIDTH
    sub = min(MIX_SUB_ROWS, tm)

    def project(r0):
        h = _normed(x_ref[0, r0:r0 + sub, :], ng1_ref[...], sh1_ref[0], sc1_ref[0])
        return _dot(h, wguv_ref[...])

    guv_next = project(0)
    for r0 in range(0, tm, sub):
        blk = slice(r0, r0 + sub)
        guv = guv_next
        if r0 + sub < tm:
            guv_next = project(r0 + sub)
        o = o_ref[0, blk, :]
        parts = []
        for h in range(HG_HEADS):
            oh = o[:, h * HG_DIM:(h + 1) * HG_DIM]
            parts.append(oh * lax.rsqrt(jnp.mean(oh * oh, axis=-1, keepdims=True) + EPS))
        o_hg = jnp.concatenate(parts, axis=-1) * hgg_ref[...] * _silu(guv[:, :w])

        u = _gelu(guv[:, w:w + MLP_WIDTH])
        v = _gelu(guv[:, w + MLP_WIDTH:])
        parts = []
        for h in range(MLP_HEADS):
            vh = v[:, h * MLP_DIM:(h + 1) * MLP_DIM]
            dv = vh - jnp.mean(vh, axis=-1, keepdims=True)
            parts.append(dv * lax.rsqrt(jnp.mean(dv * dv, axis=-1, keepdims=True) + EPS))
        vn = (jnp.concatenate(parts, axis=-1) * lng_ref[...] + lnb_ref[...]).astype(BF16)
        rows = []
        for c in range(sub // CHUNK):
            zc = [_dot(ws_ref[h], vn[c * CHUNK:(c + 1) * CHUNK, h * MLP_DIM:(h + 1) * MLP_DIM])
                  for h in range(MLP_HEADS)]
            rows.append(jnp.concatenate(zc, axis=-1) + bs_ref[...])
        o_mlp = u * jnp.concatenate(rows, axis=0)

        y = _dot(o_hg.astype(BF16), wo_ref[0:HG_WIDTH, :]) + _dot(o_mlp.astype(BF16), wo_ref[HG_WIDTH:, :])
        xn = x_ref[0, blk, :] + g1_ref[0] * y
        xo_ref[0, blk, :] = xn
        h2_ref[0, blk, :] = _normed(xn, ng2_ref[...], sh2_ref[0], sc2_ref[0])


def _mix_out_call(o, x, shift1, scale1, gain1, w_guv, hg_gain, ln_g, ln_b, w_s, bias_full, w_out,
                  gate1, shift2, scale2, gain2):
    b, t, d = x.shape
    tm = min(MIX_ROWS, t)
    w = HG_WIDTH
    tok = lambda i, j: (i, j, 0)
    mod = lambda m: pl.BlockSpec((1, 1, d), _mod_map(m, b))
    return pl.pallas_call(
        _mix_out_kernel,
        grid=(b, t // tm),
        in_specs=[
            pl.BlockSpec((1, tm, w), tok),
            pl.BlockSpec((1, tm, d), tok),
            mod(shift1), mod(scale1), _const_spec((1, d)),
            _const_spec(w_guv.shape),
            _const_spec((1, w)), _const_spec((1, w)), _const_spec((1, w)),
            _const_spec((MLP_HEADS, CHUNK, CHUNK)),
            _const_spec((CHUNK, w)),
            _const_spec((d, d)),
            mod(gate1), mod(shift2), mod(scale2),
            _const_spec((1, d)),
        ],
        out_specs=[pl.BlockSpec((1, tm, d), tok), pl.BlockSpec((1, tm, d), tok)],
        out_shape=[jax.ShapeDtypeStruct((b, t, d), F32), jax.ShapeDtypeStruct((b, t, d), BF16)],
        compiler_params=_params("parallel", "parallel"),
        name="mix_out",
    )(o, x, shift1, scale1, gain1, w_guv, hg_gain, ln_g, ln_b, w_s, bias_full, w_out, gate1, shift2, scale2, gain2)


def _ffn_kernel(*refs, two_d, final):
    if two_d:
        h_ref, hp_ref, hn_ref = refs[:3]
        refs = refs[3:]
    else:
        h_ref = refs[0]
        refs = refs[1:]
    x_ref, g2_ref, wag_ref, cw_ref, cb_ref, wd_ref = refs[:6]
    refs = refs[6:]
    if final:
        fg_ref = refs[0]
        refs = refs[1:]
    o_ref, a0, a1, g0, g1, acc_ref = refs
    a_bufs, g_bufs = (a0, a1), (g0, g1)

    tm = x_ref.shape[1]
    nchunks, fc, _ = wd_ref.shape
    mb = min(FFN_MATMUL_ROWS, tm)
    rb = min(FFN_ROW_BLOCK, tm)
    if two_d:
        has_prev = pl.program_id(1) > 0
        has_next = pl.program_id(1) < pl.num_programs(1) - 1
        row_offsets = (0, GRID_W, 2 * GRID_W)
        ext = tm + 2 * GRID_W
        main = SUBLANES + GRID_W
    else:
        row_offsets = (0,)
        ext = tm
        main = SUBLANES
    acc_ref[...] = jnp.zeros_like(acc_ref)
    for g_buf in g_bufs:
        g_buf[0:SUBLANES, :] = jnp.zeros((SUBLANES, fc), F32)
        g_buf[SUBLANES + ext:, :] = jnp.zeros((SUBLANES, fc), F32)

    def up_rows(c, slot, r0):
        g_buf = g_bufs[slot]
        h = h_ref[0, r0:r0 + mb, :]
        ag = _dot(h, wag_ref[c])
        a_bufs[slot][r0:r0 + mb, :] = ag[:, :fc]
        g_buf[main + r0:main + r0 + mb, :] = ag[:, fc:]
        if two_d and r0 == 0:
            g_buf[SUBLANES:main, :] = jnp.where(has_prev, _dot(hp_ref[0], wag_ref[c, :, fc:]), 0.0)
        if two_d and r0 + mb == tm:
            g_buf[main + tm:SUBLANES + ext, :] = jnp.where(has_next, _dot(hn_ref[0], wag_ref[c, :, fc:]), 0.0)

    def finish_rows(c, slot, r0):
        g_buf = g_bufs[slot]
        acts = []
        for q0 in range(r0, r0 + mb, rb):
            pos = q0 + lax.broadcasted_iota(jnp.int32, (rb, 1), 0)
            if two_d:
                has_left = pos % GRID_W != 0
                has_right = pos % GRID_W != GRID_W - 1
            else:
                has_left = pos != 0
                has_right = pos != tm - 1

            def taps(dc, lo, hi):
                part = None
                for ri, off in enumerate(row_offsets):
                    tap = ri * 3 + dc + 1
                    start = SUBLANES + off + q0
                    term = _tile_op(jnp.multiply, g_buf[start + lo:start + hi, :], cw_ref[c, tap])
                    part = term if part is None else part + term
                return part

            left = pltpu.roll(taps(-1, -SUBLANES, rb), 1, axis=0)[SUBLANES:]
            right = pltpu.roll(taps(1, 0, rb + SUBLANES), rb + SUBLANES - 1, axis=0)[:rb]
            conv = taps(0, 0, rb) + jnp.where(has_left, left, 0.0) + jnp.where(has_right, right, 0.0)
            act = a_bufs[slot][q0:q0 + rb, :] * _gelu(_tile_op(jnp.add, conv, cb_ref[c]))
            acts.append(act.astype(BF16))
        acc_ref[r0:r0 + mb, :] += _dot(jnp.concatenate(acts, axis=0), wd_ref[c])

    nb = tm // mb
    lead = min(2, nb)
    total = nchunks * nb

    def run_item(chunk0, rel, with_up):
        if with_up:
            co, k = divmod(rel + lead, nb)
            up_rows(chunk0 + co, co % 2, k * mb)
        co, k = divmod(rel, nb)
        finish_rows(chunk0 + co, co % 2, k * mb)

    for n in range(lead):
        up_rows(n // nb, (n // nb) % 2, (n % nb) * mb)
    npairs = (nchunks - 1) // 2

    def pair(j, carry):
        for rel in range(2 * nb):
            run_item(2 * j, rel, True)
        return carry

    lax.fori_loop(0, npairs, pair, 0)
    for n in range(npairs * 2 * nb, total):
        run_item(2 * npairs, n - npairs * 2 * nb, n + lead < total)
    y = x_ref[0] + g2_ref[0] * acc_ref[...]
    if final:
        ms = jnp.mean(y * y, axis=-1, keepdims=True)
        y = y * lax.rsqrt(ms + EPS) * fg_ref[...]
    o_ref[0] = y


def _ffn_call(h2, x, gate2, w_ag, conv_taps, conv_bias, w_d, two_d, final_gain=None):
    b, t, d = x.shape
    nchunks, fc, _ = w_d.shape
    tok = lambda i, j: (i, j, 0)
    if two_d:
        tm = min(FFN_ROWS, t)
        rpt = tm // GRID_W
        nrows = t // GRID_W
        h_specs = [
            pl.BlockSpec((1, tm, d), tok),
            pl.BlockSpec((1, GRID_W, d), lambda i, j: (i, jnp.maximum(j * rpt - 1, 0), 0)),
            pl.BlockSpec((1, GRID_W, d), lambda i, j: (i, jnp.minimum((j + 1) * rpt, nrows - 1), 0)),
        ]
        h_args = [h2, h2, h2]
        ext = tm + 2 * GRID_W
    else:
        tm = t
        h_specs = [pl.BlockSpec((1, tm, d), tok)]
        h_args = [h2]
        ext = tm
    ntaps = conv_taps.shape[1]
    in_specs = h_specs + [
        pl.BlockSpec((1, tm, d), tok),
        pl.BlockSpec((1, 1, d), _mod_map(gate2, b)),
        _const_spec((nchunks, d, 2 * fc)),
        _const_spec((nchunks, ntaps, SUBLANES, fc)), _const_spec((nchunks, SUBLANES, fc)),
        _const_spec((nchunks, fc, d)),
    ]
    args = h_args + [x, gate2, w_ag, conv_taps, conv_bias, w_d]
    if final_gain is not None:
        in_specs.append(_const_spec((1, d)))
        args.append(final_gain)
    return pl.pallas_call(
        functools.partial(_ffn_kernel, two_d=two_d, final=final_gain is not None),
        grid=(b, t // tm),
        in_specs=in_specs,
        out_specs=pl.BlockSpec((1, tm, d), tok),
        out_shape=jax.ShapeDtypeStruct((b, t, d), F32),
        scratch_shapes=[pltpu.VMEM((tm, fc), F32), pltpu.VMEM((tm, fc), F32),
                        pltpu.VMEM((ext + 2 * SUBLANES, fc), F32), pltpu.VMEM((ext + 2 * SUBLANES, fc), F32),
                        pltpu.VMEM((tm, d), F32)],
        compiler_params=_params("parallel", "parallel"),
        name="ffn" if two_d else "ffn_ctx",
    )(*args)


def _sublane_tiles(a):
    return jnp.broadcast_to(a[..., None, :], a.shape[:-1] + (SUBLANES, a.shape[-1]))


def _chunked_cols(w, fc):
    d, f = w.shape
    return w.reshape(d, f // fc, fc).transpose(1, 0, 2)


def kernel(x, c, ctx, c_ctx, w_ada, b_ada, norm_mix, norm_ffn, w_in, lb_logits_fwd, lb_logits_bwd, hg_norm,
           sgu_norm_g, sgu_norm_b, w_spatial, b_spatial, w_out, w_up, conv_w, conv_b, w_down, norm_final):
    bsz, t, d = x.shape
    depth = w_ada.shape[0]
    f = w_down.shape[1]
    w = HG_WIDTH
    assert t % 512 == 0 and ctx.shape[1] % 256 == 0 and f % FFN_CHUNK == 0

    cond_rows = -(-(bsz + 1) // 8) * 8
    cond = jnp.concatenate([c, c_ctx[None, :], jnp.zeros((cond_rows - bsz - 1, d), F32)], axis=0)
    ada = _ada_call(cond, w_ada, b_ada)

    zeros_state = jnp.zeros((bsz, HG_HEADS, HG_DIM, HG_DIM), F32)
    lbs = (lb_logits_fwd, lb_logits_bwd)
    nch = f // FFN_CHUNK
    w_ag_all = (w_up.astype(BF16).reshape(depth, d, 2, nch, FFN_CHUNK).transpose(0, 3, 1, 2, 4)
                .reshape(depth, nch, d, 2 * FFN_CHUNK))
    w_d_all = w_down.astype(BF16).reshape(depth, nch, FFN_CHUNK, d)
    xc = ctx
    for l in range(depth):
        last = l == depth - 1
        mods = ada[l].reshape(cond_rows, 6, d)
        sh1, sc1, g1, sh2, sc2, g2 = (mods[:bsz, k][:, None, :] for k in range(6))
        csh1, csc1, cg1, csh2, csc2, cg2 = (mods[bsz:bsz + 1, k][:, None, :] for k in range(6))
        w_in_l = w_in[l].astype(BF16)
        w_guv = w_in_l[:, 4 * w:]
        gain_mix = norm_mix[l][None, :]
        gain_ffn = norm_ffn[l][None, :]
        w_out_l = w_out[l].astype(BF16)
        w_ag = w_ag_all[l]
        w_d = w_d_all[l]
        taps = _sublane_tiles(_chunked_cols(conv_w[l].reshape(9, f), FFN_CHUNK))
        cbias = _sublane_tiles(conv_b[l].reshape(f // FFN_CHUNK, FFN_CHUNK))
        mix_w = (w_guv, hg_norm[l][None, :], sgu_norm_g[l][None, :], sgu_norm_b[l][None, :],
                 w_spatial[l].astype(BF16), jnp.repeat(b_spatial[l].T, MLP_DIM, axis=1),
                 w_out_l)

        if last:
            pc = _in_proj_call(xc, csh1, csc1, gain_mix, w_in_l, 3 * w)
            hf_c, hb_c = _scan_call(pc, *lbs, zeros_state, zeros_state, l, False)
        else:
            pc = _in_proj_call(xc, csh1, csc1, gain_mix, w_in_l, 4 * w)
            oc, hf_c, hb_c = _scan_call(pc, *lbs, zeros_state, zeros_state, l, True)

        p = _in_proj_call(x, sh1, sc1, gain_mix, w_in_l, 4 * w)
        o, _, _ = _scan_call(p, *lbs, hf_c, hb_c, l, True)
        x_mid, h2 = _mix_out_call(o, x, sh1, sc1, gain_mix, *mix_w, g1, sh2, sc2, gain_ffn)
        x = _ffn_call(h2, x_mid, g2, w_ag, taps, cbias, w_d, True,
                      final_gain=norm_final[None, :] if last else None)

        if not last:
            xc_mid, hc2 = _mix_out_call(oc, xc, csh1, csc1, gain_mix, *mix_w, cg1, csh2, csc2, gain_ffn)
            xc = _ffn_call(hc2, xc_mid, cg2, w_ag, taps[:, 3:6], cbias, w_d, False)
    return x
```

```python
import functools

import jax
import jax.numpy as jnp
from jax import lax
from jax.experimental import pallas as pl
from jax.experimental.pallas import tpu as pltpu

F32 = jnp.float32
BF16 = jnp.bfloat16

HG_DIM = 128
HG_HEADS = 4
HG_WIDTH = HG_DIM * HG_HEADS
MLP_HEADS = 4
MLP_DIM = 128
MLP_WIDTH = MLP_HEADS * MLP_DIM
GRID_W = 64
CHUNK = 128
SCAN_HEADS_PER_PASS = 2
EPS = 1e-6
Q_SCALE = HG_DIM ** -0.5
SQRT_HALF = 0.7071067811865476
LOG2_E = 1.4426950408889634

IN_PROJ_ROWS = 512
IN_PROJ_COLS = 512
FFN_CHUNK = 256
FFN_ROWS = 1024
FFN_ROW_BLOCK = 128
FFN_MATMUL_ROWS = 256
MIX_ROWS = 1024
MIX_SUB_ROWS = 256
SUBLANES = 8
VMEM_LIMIT = 56 * 1024 * 1024


def _const_spec(shape):
    zeros = (0,) * len(shape)
    return pl.BlockSpec(shape, lambda *_: zeros, pipeline_mode=pl.Buffered(1))


def _params(*semantics):
    return pltpu.CompilerParams(dimension_semantics=semantics, vmem_limit_bytes=VMEM_LIMIT)


def _mod_map(mod, batch):
    if mod.shape[0] == batch and batch > 1:
        return lambda i, j: (i, 0, 0)
    return lambda i, j: (0, 0, 0)


def _gelu(x):
    return 0.5 * x * (1.0 + lax.erf(x * SQRT_HALF))


def _silu(x):
    return x * jax.nn.sigmoid(x)


def _dot(a, b):
    return jnp.dot(a, b, preferred_element_type=F32)


def _dot_nt(a, b):
    return lax.dot_general(a, b, (((1,), (1,)), ((), ())), preferred_element_type=F32)


def _tile_op(op, x, tile):
    rows, n = x.shape
    return op(x.reshape(rows // SUBLANES, SUBLANES, n), tile[None]).reshape(rows, n)


def _normed(x, gain, shift, scale):
    ms = jnp.mean(x * x, axis=-1, keepdims=True)
    h = x * lax.rsqrt(ms + EPS) * gain
    return (h * (1.0 + scale) + shift).astype(BF16)


def _ada_kernel(c_ref, w_ref, b_ref, o_ref):
    a = _silu(c_ref[...])
    a_hi = a.astype(BF16)
    a_lo = (a - a_hi.astype(F32)).astype(BF16)
    w = w_ref[0]
    w_hi = w.astype(BF16)
    w_lo = (w - w_hi.astype(F32)).astype(BF16)
    o_ref[0] = _dot(a_hi, w_hi) + _dot(a_hi, w_lo) + _dot(a_lo, w_hi) + b_ref[0]


def _ada_call(cc, w_ada, b_ada):
    depth, d, e = w_ada.shape
    rows = cc.shape[0]
    tn = 1536
    return pl.pallas_call(
        _ada_kernel,
        grid=(depth, e // tn),
        in_specs=[
            pl.BlockSpec((rows, d), lambda l, j: (0, 0)),
            pl.BlockSpec((1, d, tn), lambda l, j: (l, 0, j)),
            pl.BlockSpec((1, 1, tn), lambda l, j: (l, 0, j)),
        ],
        out_specs=pl.BlockSpec((1, rows, tn), lambda l, j: (l, 0, j)),
        out_shape=jax.ShapeDtypeStruct((depth, rows, e), F32),
        compiler_params=_params("parallel", "parallel"),
        name="ada",
    )(cc, w_ada, b_ada.reshape(depth, 1, e))


def _in_proj_kernel(x_ref, sh_ref, sc_ref, gain_ref, w_ref, o_ref):
    h = _normed(x_ref[0], gain_ref[...], sh_ref[0], sc_ref[0])
    for c0 in range(0, o_ref.shape[2], IN_PROJ_COLS):
        cols = slice(c0, c0 + IN_PROJ_COLS)
        o_ref[0, :, cols] = _dot(h, w_ref[:, cols])


def _in_proj_call(x, shift, scale, gain, w, ncols):
    b, t, d = x.shape
    tm = min(IN_PROJ_ROWS, t)
    return pl.pallas_call(
        _in_proj_kernel,
        grid=(b, t // tm),
        in_specs=[
            pl.BlockSpec((1, tm, d), lambda i, j: (i, j, 0)),
            pl.BlockSpec((1, 1, d), _mod_map(shift, b)),
            pl.BlockSpec((1, 1, d), _mod_map(scale, b)),
            _const_spec((1, d)),
            pl.BlockSpec((d, ncols), lambda i, j: (0, 0), pipeline_mode=pl.Buffered(1)),
        ],
        out_specs=pl.BlockSpec((1, tm, ncols), lambda i, j: (i, j, 0)),
        out_shape=jax.ShapeDtypeStruct((b, t, ncols), F32),
        compiler_params=_params("parallel", "parallel"),
        name="in_proj",
    )(x, shift, scale, gain, w)


def _low_level_exponent(b2, logf2, m, reverse):
    n, w = b2.shape
    pos = lax.broadcasted_iota(jnp.int32, (n, 1), 0)
    if m == 1:
        sel = (pos % 2 == 0) if reverse else (pos % 2 == 1)
        return jnp.where(sel, logf2, 0.0)
    if m == 2:
        r = pos % 4
        nxt = pltpu.roll(logf2, n - 1, axis=0)
        prv = pltpu.roll(logf2, 1, axis=0)
        if reverse:
            return jnp.where(r == 0, logf2 + nxt, jnp.where(r == 1, logf2, jnp.where(r == 2, 0.0, prv)))
        return jnp.where(r == 0, nxt, jnp.where(r == 1, 0.0, jnp.where(r == 2, logf2, logf2 + prv)))
    assert m == 4
    b3 = b2.reshape(n // SUBLANES, SUBLANES, w)
    mid = m if reverse else m - 1
    return (-jnp.abs(b3 - b3[:, mid:mid + 1, :])).reshape(n, w)


def _scan_direction(f_logit, v, q_raw, lb, tri, lvl, st_ref, o_ref, row0, b_scr, head0, *, first, reverse):
    n, w = f_logit.shape
    if first:
        e = jnp.exp(-jnp.abs(f_logit))
        logf = jnp.minimum(f_logit, 0.0) - jnp.log1p(e)
        kk = jnp.where(f_logit >= 0.0, e, 1.0) / (1.0 + e)
    else:
        gate = lb + (1.0 - lb) * jax.nn.sigmoid(f_logit)
        logf = jnp.log(gate)
        kk = 1.0 - gate
    logf2 = logf * LOG2_E
    hi = logf2.astype(BF16)
    lo = (logf2 - hi.astype(F32)).astype(BF16)
    b2 = _dot(tri, hi) + _dot(tri, lo)
    last2 = b2[0:1, :] if reverse else b2[n - 1:n, :]
    k16 = kk.astype(BF16)
    k_end = k16 * jnp.exp2(last2 - b2).astype(BF16)
    state_decay = jnp.exp2(last2)
    v16 = v.astype(BF16)

    if o_ref is not None:
        q = _silu(q_raw) * Q_SCALE
        q16 = q.astype(BF16)
        q_in = q16 * jnp.exp2(b2).astype(BF16)
        heads = [slice(h * HG_DIM, (h + 1) * HG_DIM) for h in range(w // HG_DIM)]
        diag_code = n.bit_length() - 1
        on_diag = lvl == diag_code
        scores = [jnp.where(on_diag, _dot_nt(q16[:, hs], k16[:, hs]), 0.0) for hs in heads]

        m, code = 1, 0
        while 2 * m <= SUBLANES:
            ex = jnp.exp2(_low_level_exponent(b2, logf2, m, reverse)).astype(BF16)
            q_m = q16 * ex
            k_m = k16 * ex
            at_level = lvl == code
            for h, hs in enumerate(heads):
                scores[h] = jnp.where(at_level, _dot_nt(q_m[:, hs], k_m[:, hs]), scores[h])
            m, code = 2 * m, code + 1

        b_scr[:, 0:w] = b2
        while m < n:
            bases = range(0, n, 2 * m)
            q_parts, k_parts = [], []
            for base in bases:
                first_half, second_half = slice(base, base + m), slice(base + m, base + 2 * m)
                t_rows, s_rows = (first_half, second_half) if reverse else (second_half, first_half)
                r = base + (m if reverse else m - 1)
                bref = b_scr[r:r + 1, 0:w]
                q_parts.append(q[t_rows] * jnp.exp2(b2[t_rows] - bref))
                k_s = kk[s_rows] * jnp.exp2(bref - b2[s_rows])
                k_parts += [jnp.zeros_like(k_s), k_s] if reverse else [k_s, jnp.zeros_like(k_s)]
            at_level = lvl == code
            q_t = jnp.concatenate(q_parts, axis=0).astype(BF16)
            k_s = jnp.concatenate(k_parts, axis=0).astype(BF16)
            for h, hs in enumerate(heads):
                s_t = _dot_nt(q_t[:, hs], k_s[:, hs])
                rows = []
                for j, base in enumerate(bases):
                    first_half, second_half = slice(base, base + m), slice(base + m, base + 2 * m)
                    t_rows, s_rows = (first_half, second_half) if reverse else (second_half, first_half)
                    new = jnp.where(at_level[t_rows], s_t[j * m:(j + 1) * m], scores[h][t_rows])
                    rows += [new, scores[h][s_rows]] if reverse else [scores[h][s_rows], new]
                scores[h] = jnp.concatenate(rows, axis=0)
            m, code = 2 * m, code + 1

    for h in range(w // HG_DIM):
        hs = slice(h * HG_DIM, (h + 1) * HG_DIM)
        out = slice((head0 + h) * HG_DIM, (head0 + h + 1) * HG_DIM)
        st = st_ref[0, head0 + h]
        if o_ref is not None:
            o_ref[0, pl.ds(row0, n), out] += (_dot(scores[h].astype(BF16), v16[:, hs])
                                              + _dot_nt(q_in[:, hs], st.astype(BF16)))
        v_t = v[:, hs].T.astype(BF16)
        st_ref[0, head0 + h] = st * state_decay[:, hs] + _dot(v_t, k_end[:, hs])


def _lower_bound(logits_ref, layer):
    z = logits_ref[...]
    ez = jnp.exp(z - jnp.max(z, axis=0, keepdims=True))
    sm = ez / jnp.sum(ez, axis=0, keepdims=True)
    lb = jnp.zeros((1, z.shape[1]), F32)
    for j in range(1, layer + 1):
        lb = lb + sm[j:j + 1, :]
    return lb


def _scan_kernel(*refs, layer, want_o):
    if want_o:
        (ff_ref, fb_ref, vf_ref, vb_ref, qf_ref, qb_ref, lgf_ref, lgb_ref, trif_ref, trib_ref, lvlf_ref, lvlb_ref,
         h0f_ref, h0b_ref, o_ref, hf_ref, hb_ref, b_scr) = refs
    else:
        (ff_ref, fb_ref, vf_ref, vb_ref, lgf_ref, lgb_ref, trif_ref, trib_ref,
         h0f_ref, h0b_ref, hf_ref, hb_ref) = refs
        qf_ref = qb_ref = o_ref = lvlf_ref = lvlb_ref = b_scr = None
    step = pl.program_id(1)

    @pl.when(step == 0)
    def _():
        hf_ref[...] = h0f_ref[...]
        hb_ref[...] = h0b_ref[...]
        if want_o:
            o_ref[...] = jnp.zeros_like(o_ref)

    first = layer == 0
    fwd_row0 = pl.multiple_of(step * CHUNK, CHUNK)
    bwd_row0 = pl.multiple_of((pl.num_programs(1) - 1 - step) * CHUNK, CHUNK)
    for reverse, f_ref, v_ref, q_ref, lg_ref, tri_ref, lvl_ref, st_ref, row0 in (
            (False, ff_ref, vf_ref, qf_ref, lgf_ref, trif_ref, lvlf_ref, hf_ref, fwd_row0),
            (True, fb_ref, vb_ref, qb_ref, lgb_ref, trib_ref, lvlb_ref, hb_ref, bwd_row0)):
        lb = None if first else _lower_bound(lg_ref, layer)
        for head0 in range(0, HG_HEADS, SCAN_HEADS_PER_PASS):
            cols = slice(head0 * HG_DIM, (head0 + SCAN_HEADS_PER_PASS) * HG_DIM)
            _scan_direction(
                f_ref[0, :, cols], v_ref[0, :, cols], q_ref[0, :, cols] if want_o else None,
                None if first else lb[:, cols], tri_ref[...], lvl_ref[...] if want_o else None,
                st_ref, o_ref, row0, b_scr, head0, first=first, reverse=reverse)


def _scan_masks(n):
    t = jnp.arange(n, dtype=jnp.int32)[:, None]
    s = jnp.arange(n, dtype=jnp.int32)[None, :]
    x = t ^ s
    level = jnp.zeros((n, n), jnp.int32)
    for k in range(1, n.bit_length() - 1):
        level = jnp.where((x >> k) > 0, k, level)
    diag_code = n.bit_length() - 1
    lvl_f = jnp.where(t == s, diag_code, jnp.where(t > s, level, -1))
    return (t >= s).astype(BF16), (t <= s).astype(BF16), lvl_f, lvl_f.T


def _scan_call(p, lb_logits_f, lb_logits_b, h0f, h0b, layer, want_o):
    b, t, _ = p.shape
    nc = t // CHUNK
    w = HG_WIDTH
    tri_f, tri_b, lvl_f, lvl_b = _scan_masks(CHUNK)
    depth = lb_logits_f.shape[0]

    def col(block, rev):
        if rev:
            return pl.BlockSpec((1, CHUNK, w), lambda i, j: (i, nc - 1 - j, block))
        return pl.BlockSpec((1, CHUNK, w), lambda i, j: (i, j, block))

    state_spec = pl.BlockSpec((1, HG_HEADS, HG_DIM, HG_DIM), lambda i, j: (i, 0, 0, 0))
    state_shape = jax.ShapeDtypeStruct((b, HG_HEADS, HG_DIM, HG_DIM), F32)
    in_specs = [col(0, False), col(1, True), col(2, False), col(2, True)]
    args = [p, p, p, p]
    if want_o:
        in_specs += [col(3, False), col(3, True)]
        args += [p, p]
    in_specs += [_const_spec((depth, w)), _const_spec((depth, w)),
                 _const_spec((CHUNK, CHUNK)), _const_spec((CHUNK, CHUNK))]
    args += [lb_logits_f, lb_logits_b, tri_f, tri_b]
    if want_o:
        in_specs += [_const_spec((CHUNK, CHUNK)), _const_spec((CHUNK, CHUNK))]
        args += [lvl_f, lvl_b]
    in_specs += [state_spec, state_spec]
    args += [h0f, h0b]
    out_specs = [state_spec, state_spec]
    out_shape = [state_shape, state_shape]
    scratch = []
    if want_o:
        out_specs = [pl.BlockSpec((1, t, w), lambda i, j: (i, 0, 0))] + out_specs
        out_shape = [jax.ShapeDtypeStruct((b, t, w), F32)] + out_shape
        scratch = [pltpu.VMEM((CHUNK, w), F32)]
    return pl.pallas_call(
        functools.partial(_scan_kernel, layer=layer, want_o=want_o),
        grid=(b, nc),
        in_specs=in_specs,
        out_specs=out_specs,
        out_shape=out_shape,
        scratch_shapes=scratch,
        compiler_params=_params("parallel", "arbitrary"),
        name="scan" if want_o else "scan_state",
    )(*args)


def _mix_out_kernel(o_ref, x_ref, sh1_ref, sc1_ref, ng1_ref, wguv_ref, hgg_ref, lng_ref, lnb_ref, ws_ref, bs_ref,
                    wo_ref, g1_ref, sh2_ref, sc2_ref, ng2_ref, xo_ref, h2_ref):
    tm = x_ref.shape[1]
    w = HG_WIDTH
    sub = min(MIX_SUB_ROWS, tm)

    def project(r0):
        h = _normed(x_ref[0, r0:r0 + sub, :], ng1_ref[...], sh1_ref[0], sc1_ref[0])
        return _dot(h, wguv_ref[...])

    guv_next = project(0)
    for r0 in range(0, tm, sub):
        blk = slice(r0, r0 + sub)
        guv = guv_next
        if r0 + sub < tm:
            guv_next = project(r0 + sub)
        o = o_ref[0, blk, :]
        parts = []
        for h in range(HG_HEADS):
            oh = o[:, h * HG_DIM:(h + 1) * HG_DIM]
            parts.append(oh * lax.rsqrt(jnp.mean(oh * oh, axis=-1, keepdims=True) + EPS))
        o_hg = jnp.concatenate(parts, axis=-1) * hgg_ref[...] * _silu(guv[:, :w])

        u = _gelu(guv[:, w:w + MLP_WIDTH])
        v = _gelu(guv[:, w + MLP_WIDTH:])
        parts = []
        for h in range(MLP_HEADS):
            vh = v[:, h * MLP_DIM:(h + 1) * MLP_DIM]
            dv = vh - jnp.mean(vh, axis=-1, keepdims=True)
            parts.append(dv * lax.rsqrt(jnp.mean(dv * dv, axis=-1, keepdims=True) + EPS))
        vn = (jnp.concatenate(parts, axis=-1) * lng_ref[...] + lnb_ref[...]).astype(BF16)
        rows = []
        for c in range(sub // CHUNK):
            zc = [_dot(ws_ref[h], vn[c * CHUNK:(c + 1) * CHUNK, h * MLP_DIM:(h + 1) * MLP_DIM])
                  for h in range(MLP_HEADS)]
            rows.append(jnp.concatenate(zc, axis=-1) + bs_ref[...])
        o_mlp = u * jnp.concatenate(rows, axis=0)

        y = _dot(o_hg.astype(BF16), wo_ref[0:HG_WIDTH, :]) + _dot(o_mlp.astype(BF16), wo_ref[HG_WIDTH:, :])
        xn = x_ref[0, blk, :] + g1_ref[0] * y
        xo_ref[0, blk, :] = xn
        h2_ref[0, blk, :] = _normed(xn, ng2_ref[...], sh2_ref[0], sc2_ref[0])


def _mix_out_call(o, x, shift1, scale1, gain1, w_guv, hg_gain, ln_g, ln_b, w_s, bias_full, w_out,
                  gate1, shift2, scale2, gain2):
    b, t, d = x.shape
    tm = min(MIX_ROWS, t)
    w = HG_WIDTH
    tok = lambda i, j: (i, j, 0)
    mod = lambda m: pl.BlockSpec((1, 1, d), _mod_map(m, b))
    return pl.pallas_call(
        _mix_out_kernel,
        grid=(b, t // tm),
        in_specs=[
            pl.BlockSpec((1, tm, w), tok),
            pl.BlockSpec((1, tm, d), tok),
            mod(shift1), mod(scale1), _const_spec((1, d)),
            _const_spec(w_guv.shape),
            _const_spec((1, w)), _const_spec((1, w)), _const_spec((1, w)),
            _const_spec((MLP_HEADS, CHUNK, CHUNK)),
            _const_spec((CHUNK, w)),
            _const_spec((d, d)),
            mod(gate1), mod(shift2), mod(scale2),
            _const_spec((1, d)),
        ],
        out_specs=[pl.BlockSpec((1, tm, d), tok), pl.BlockSpec((1, tm, d), tok)],
        out_shape=[jax.ShapeDtypeStruct((b, t, d), F32), jax.ShapeDtypeStruct((b, t, d), BF16)],
        compiler_params=_params("parallel", "parallel"),
        name="mix_out",
    )(o, x, shift1, scale1, gain1, w_guv, hg_gain, ln_g, ln_b, w_s, bias_full, w_out, gate1, shift2, scale2, gain2)


def _ffn_kernel(*refs, two_d, final):
    if two_d:
        h_ref, hp_ref, hn_ref = refs[:3]
        refs = refs[3:]
    else:
        h_ref = refs[0]
        refs = refs[1:]
    x_ref, g2_ref, wag_ref, cw_ref, cb_ref, wd_ref = refs[:6]
    refs = refs[6:]
    if final:
        fg_ref = refs[0]
        refs = refs[1:]
    o_ref, a0, a1, g0, g1, acc_ref = refs
    a_bufs, g_bufs = (a0, a1), (g0, g1)

    tm = x_ref.shape[1]
    nchunks, fc, _ = wd_ref.shape
    mb = min(FFN_MATMUL_ROWS, tm)
    rb = min(FFN_ROW_BLOCK, tm)
    if two_d:
        has_prev = pl.program_id(1) > 0
        has_next = pl.program_id(1) < pl.num_programs(1) - 1
        row_offsets = (0, GRID_W, 2 * GRID_W)
        ext = tm + 2 * GRID_W
        main = SUBLANES + GRID_W
    else:
        row_offsets = (0,)
        ext = tm
        main = SUBLANES
    acc_ref[...] = jnp.zeros_like(acc_ref)
    for g_buf in g_bufs:
        g_buf[0:SUBLANES, :] = jnp.zeros((SUBLANES, fc), F32)
        g_buf[SUBLANES + ext:, :] = jnp.zeros((SUBLANES, fc), F32)

    def up_rows(c, slot, r0):
        g_buf = g_bufs[slot]
        h = h_ref[0, r0:r0 + mb, :]
        ag = _dot(h, wag_ref[c])
        a_bufs[slot][r0:r0 + mb, :] = ag[:, :fc]
        g_buf[main + r0:main + r0 + mb, :] = ag[:, fc:]
        if two_d and r0 == 0:
            g_buf[SUBLANES:main, :] = jnp.where(has_prev, _dot(hp_ref[0], wag_ref[c, :, fc:]), 0.0)
        if two_d and r0 + mb == tm:
            g_buf[main + tm:SUBLANES + ext, :] = jnp.where(has_next, _dot(hn_ref[0], wag_ref[c, :, fc:]), 0.0)

    def finish_rows(c, slot, r0):
        g_buf = g_bufs[slot]
        acts = []
        for q0 in range(r0, r0 + mb, rb):
            pos = q0 + lax.broadcasted_iota(jnp.int32, (rb, 1), 0)
            if two_d:
                has_left = pos % GRID_W != 0
                has_right = pos % GRID_W != GRID_W - 1
            else:
                has_left = pos != 0
                has_right = pos != tm - 1

            def taps(dc, lo, hi):
                part = None
                for ri, off in enumerate(row_offsets):
                    tap = ri * 3 + dc + 1
                    start = SUBLANES + off + q0
                    term = _tile_op(jnp.multiply, g_buf[start + lo:start + hi, :], cw_ref[c, tap])
                    part = term if part is None else part + term
                return part

            left = pltpu.roll(taps(-1, -SUBLANES, rb), 1, axis=0)[SUBLANES:]
            right = pltpu.roll(taps(1, 0, rb + SUBLANES), rb + SUBLANES - 1, axis=0)[:rb]
            conv = taps(0, 0, rb) + jnp.where(has_left, left, 0.0) + jnp.where(has_right, right, 0.0)
            act = a_bufs[slot][q0:q0 + rb, :] * _gelu(_tile_op(jnp.add, conv, cb_ref[c]))
            acts.append(act.astype(BF16))
        acc_ref[r0:r0 + mb, :] += _dot(jnp.concatenate(acts, axis=0), wd_ref[c])

    nb = tm // mb
    lead = min(2, nb)
    total = nchunks * nb

    def run_item(chunk0, rel, with_up):
        if with_up:
            co, k = divmod(rel + lead, nb)
            up_rows(chunk0 + co, co % 2, k * mb)
        co, k = divmod(rel, nb)
        finish_rows(chunk0 + co, co % 2, k * mb)

    for n in range(lead):
        up_rows(n // nb, (n // nb) % 2, (n % nb) * mb)
    npairs = (nchunks - 1) // 2

    def pair(j, carry):
        for rel in range(2 * nb):
            run_item(2 * j, rel, True)
        return carry

    lax.fori_loop(0, npairs, pair, 0)
    for n in range(npairs * 2 * nb, total):
        run_item(2 * npairs, n - npairs * 2 * nb, n + lead < total)
    y = x_ref[0] + g2_ref[0] * acc_ref[...]
    if final:
        ms = jnp.mean(y * y, axis=-1, keepdims=True)
        y = y * lax.rsqrt(ms + EPS) * fg_ref[...]
    o_ref[0] = y


def _ffn_call(h2, x, gate2, w_ag, conv_taps, conv_bias, w_d, two_d, final_gain=None):
    b, t, d = x.shape
    nchunks, fc, _ = w_d.shape
    tok = lambda i, j: (i, j, 0)
    if two_d:
        tm = min(FFN_ROWS, t)
        rpt = tm // GRID_W
        nrows = t // GRID_W
        h_specs = [
            pl.BlockSpec((1, tm, d), tok),
            pl.BlockSpec((1, GRID_W, d), lambda i, j: (i, jnp.maximum(j * rpt - 1, 0), 0)),
            pl.BlockSpec((1, GRID_W, d), lambda i, j: (i, jnp.minimum((j + 1) * rpt, nrows - 1), 0)),
        ]
        h_args = [h2, h2, h2]
        ext = tm + 2 * GRID_W
    else:
        tm = t
        h_specs = [pl.BlockSpec((1, tm, d), tok)]
        h_args = [h2]
        ext = tm
    ntaps = conv_taps.shape[1]
    in_specs = h_specs + [
        pl.BlockSpec((1, tm, d), tok),
        pl.BlockSpec((1, 1, d), _mod_map(gate2, b)),
        _const_spec((nchunks, d, 2 * fc)),
        _const_spec((nchunks, ntaps, SUBLANES, fc)), _const_spec((nchunks, SUBLANES, fc)),
        _const_spec((nchunks, fc, d)),
    ]
    args = h_args + [x, gate2, w_ag, conv_taps, conv_bias, w_d]
    if final_gain is not None:
        in_specs.append(_const_spec((1, d)))
        args.append(final_gain)
    return pl.pallas_call(
        functools.partial(_ffn_kernel, two_d=two_d, final=final_gain is not None),
        grid=(b, t // tm),
        in_specs=in_specs,
        out_specs=pl.BlockSpec((1, tm, d), tok),
        out_shape=jax.ShapeDtypeStruct((b, t, d), F32),
        scratch_shapes=[pltpu.VMEM((tm, fc), F32), pltpu.VMEM((tm, fc), F32),
                        pltpu.VMEM((ext + 2 * SUBLANES, fc), F32), pltpu.VMEM((ext + 2 * SUBLANES, fc), F32),
                        pltpu.VMEM((tm, d), F32)],
        compiler_params=_params("parallel", "parallel"),
        name="ffn" if two_d else "ffn_ctx",
    )(*args)


def _sublane_tiles(a):
    return jnp.broadcast_to(a[..., None, :], a.shape[:-1] + (SUBLANES, a.shape[-1]))


def _chunked_cols(w, fc):
    d, f = w.shape
    return w.reshape(d, f // fc, fc).transpose(1, 0, 2)


def kernel(x, c, ctx, c_ctx, w_ada, b_ada, norm_mix, norm_ffn, w_in, lb_logits_fwd, lb_logits_bwd, hg_norm,
           sgu_norm_g, sgu_norm_b, w_spatial, b_spatial, w_out, w_up, conv_w, conv_b, w_down, norm_final):
    bsz, t, d = x.shape
    depth = w_ada.shape[0]
    f = w_down.shape[1]
    w = HG_WIDTH
    assert t % 512 == 0 and ctx.shape[1] % 256 == 0 and f % FFN_CHUNK == 0

    cond_rows = -(-(bsz + 1) // 8) * 8
    cond = jnp.concatenate([c, c_ctx[None, :], jnp.zeros((cond_rows - bsz - 1, d), F32)], axis=0)
    ada = _ada_call(cond, w_ada, b_ada)

    zeros_state = jnp.zeros((bsz, HG_HEADS, HG_DIM, HG_DIM), F32)
    lbs = (lb_logits_fwd, lb_logits_bwd)
    xc = ctx
    for l in range(depth):
        last = l == depth - 1
        mods = ada[l].reshape(cond_rows, 6, d)
        sh1, sc1, g1, sh2, sc2, g2 = (mods[:bsz, k][:, None, :] for k in range(6))
        csh1, csc1, cg1, csh2, csc2, cg2 = (mods[bsz:bsz + 1, k][:, None, :] for k in range(6))
        w_in_l = w_in[l].astype(BF16)
        w_guv = w_in_l[:, 4 * w:]
        gain_mix = norm_mix[l][None, :]
        gain_ffn = norm_ffn[l][None, :]
        w_out_l = w_out[l].astype(BF16)
        nch = f // FFN_CHUNK
        w_ag = (w_up[l].astype(BF16).reshape(d, 2, nch, FFN_CHUNK).transpose(2, 0, 1, 3)
                .reshape(nch, d, 2 * FFN_CHUNK))
        w_d = w_down[l].astype(BF16).reshape(f // FFN_CHUNK, FFN_CHUNK, d)
        taps = _sublane_tiles(_chunked_cols(conv_w[l].reshape(9, f), FFN_CHUNK))
        cbias = _sublane_tiles(conv_b[l].reshape(f // FFN_CHUNK, FFN_CHUNK))
        mix_w = (w_guv, hg_norm[l][None, :], sgu_norm_g[l][None, :], sgu_norm_b[l][None, :],
                 w_spatial[l].astype(BF16), jnp.repeat(b_spatial[l].T, MLP_DIM, axis=1),
                 w_out_l)

        if last:
            pc = _in_proj_call(xc, csh1, csc1, gain_mix, w_in_l, 3 * w)
            hf_c, hb_c = _scan_call(pc, *lbs, zeros_state, zeros_state, l, False)
        else:
            pc = _in_proj_call(xc, csh1, csc1, gain_mix, w_in_l, 4 * w)
            oc, hf_c, hb_c = _scan_call(pc, *lbs, zeros_state, zeros_state, l, True)

        p = _in_proj_call(x, sh1, sc1, gain_mix, w_in_l, 4 * w)
        o, _, _ = _scan_call(p, *lbs, hf_c, hb_c, l, True)
        x_mid, h2 = _mix_out_call(o, x, sh1, sc1, gain_mix, *mix_w, g1, sh2, sc2, gain_ffn)
        x = _ffn_call(h2, x_mid, g2, w_ag, taps, cbias, w_d, True,
                      final_gain=norm_final[None, :] if last else None)

        if not last:
            xc_mid, hc2 = _mix_out_call(oc, xc, csh1, csc1, gain_mix, *mix_w, cg1, csh2, csc2, gain_ffn)
            xc = _ffn_call(hc2, xc_mid, cg2, w_ag, taps[:, 3:6], cbias, w_d, False)
    return x
```
